```python
import math
import jax, jax.numpy as jnp
from jax import lax
import numpy as np

D_MODEL = 2048
BATCH = 4
SEQ = 2048
DEPTH = 2
DEC_BATCH = 32
DEC_SEQ = 4
PAST_LEN = 16384
PAGE_SIZE = 128

DN_HK = 16
DN_HV = 32
DN_DK = 128
DN_DV = 128
DN_CONV = 4
DN_CHUNK = 64
SW_HQ = 32
SW_HKV = 8
SW_HD = 64
WINDOW = 128
SW_QBLOCK = 128
ROT_DIM = SW_HD // 4
ROPE_THETA = 500000.0
PEER_HEADS = 8
PEER_NKEYS = 128
PEER_EXPERTS = PEER_NKEYS * PEER_NKEYS
PEER_DKEY = 256
PEER_TOPK = 16
PEER_TBLOCK = 128
EPS = 1e-6
NEG = -1e30

DN_QW = DN_HK * DN_DK
DN_VW = DN_HV * DN_DV
CONV_CH = 2 * DN_QW + DN_VW
SW_QW = SW_HQ * SW_HD
SW_KW = SW_HKV * SW_HD
IN_SPLITS = (CONV_CH, DN_VW, DN_HV, DN_HV, SW_QW, SW_KW, SW_KW, D_MODEL, D_MODEL)
IN_COLS = sum(IN_SPLITS)

kernel_name = 'hybrid_deltanet_swa_peer_step'


def split_cols(x, sizes):
    offs = np.cumsum(sizes)[:-1].tolist()
    return jnp.split(x, offs, axis=-1)


def rmsnorm(x, g):
    xf = x.astype(jnp.float32)
    y = xf * lax.rsqrt(jnp.mean(xf * xf, axis=-1, keepdims=True) + EPS)
    return (y * g.astype(jnp.float32)).astype(x.dtype)


def l2norm(x):
    xf = x.astype(jnp.float32)
    return xf * lax.rsqrt(jnp.sum(xf * xf, axis=-1, keepdims=True) + EPS)


def rotary(x, pos):
    half = ROT_DIM // 2
    inv = 1.0 / (ROPE_THETA ** (jnp.arange(0, ROT_DIM, 2, dtype=jnp.float32) / ROT_DIM))
    ang = pos[:, None] * inv[None, :]
    cos = jnp.cos(ang)[:, None, :]
    sin = jnp.sin(ang)[:, None, :]
    xr = x[..., :ROT_DIM].astype(jnp.float32)
    x1, x2 = xr[..., :half], xr[..., half:]
    rot = jnp.concatenate([x1 * cos - x2 * sin, x2 * cos + x1 * sin], axis=-1)
    return jnp.concatenate([rot.astype(x.dtype), x[..., ROT_DIM:]], axis=-1)


def gated_delta_chunked(q, k, v, g, beta, s0):
    B, L, H, _ = q.shape
    C = min(DN_CHUNK, L)
    n = -(-L // C)
    pad = n * C - L

    def chunks(t):
        t = jnp.pad(t, [(0, 0), (0, pad)] + [(0, 0)] * (t.ndim - 2))
        t = t.reshape((B, n, C) + t.shape[2:])
        return jnp.moveaxis(t, 2, 3)

    qc, kc, vc, gch, bc = chunks(q), chunks(k), chunks(v), chunks(g), chunks(beta)
    gcum = jnp.cumsum(gch, axis=-1)
    idx = jnp.arange(C)
    incl = idx[:, None] >= idx[None, :]
    strict = idx[:, None] > idx[None, :]
    decay = jnp.exp(jnp.where(incl, gcum[..., :, None] - gcum[..., None, :], -jnp.inf))
    kk = jnp.einsum('bnhid,bnhjd->bnhij', kc, kc)
    a_mat = jnp.where(strict, bc[..., :, None] * kk * decay, 0.0) + jnp.eye(C, dtype=jnp.float32)
    rhs = jnp.concatenate([bc[..., None] * vc, (bc * jnp.exp(gcum))[..., None] * kc], axis=-1)
    sol = lax.linalg.triangular_solve(a_mat, rhs, left_side=True, lower=True, unit_diagonal=True)
    u0, w = sol[..., :DN_DV], sol[..., DN_DV:]
    qk = jnp.einsum('bnhid,bnhjd->bnhij', qc, kc) * decay
    qg = qc * jnp.exp(gcum)[..., None]
    glast = gcum[..., -1]
    ktail = kc * jnp.exp(glast[..., None] - gcum)[..., None]

    def step(S, xs):
        u0_, w_, qg_, qk_, kt_, gl_ = xs
        u = u0_ - jnp.einsum('bhck,bhkv->bhcv', w_, S)
        o = jnp.einsum('bhck,bhkv->bhcv', qg_, S) + jnp.einsum('bhij,bhjv->bhiv', qk_, u)
        S = jnp.exp(gl_)[..., None, None] * S + jnp.einsum('bhck,bhcv->bhkv', kt_, u)
        return S, o

    xs = tuple(jnp.moveaxis(t, 1, 0) for t in (u0, w, qg, qk, ktail, glast))
    s_fin, o = lax.scan(step, s0, xs)
    o = jnp.moveaxis(jnp.moveaxis(o, 0, 1), 2, 3).reshape(B, n * C, H, DN_DV)[:, :L]
    return o, s_fin


def swa_banded(q, k_cat, v_cat, sinks, pos0, wb):
    B, L = q.shape[:2]
    G = SW_HQ // SW_HKV
    QB = min(SW_QBLOCK, L)
    nb = -(-L // QB)
    Lp = nb * QB
    qp = jnp.pad(q, [(0, 0), (0, Lp - L), (0, 0), (0, 0)]).reshape(B, nb, QB, SW_HKV, G, SW_HD)
    kp = jnp.pad(k_cat, [(0, 0), (0, Lp - L), (0, 0), (0, 0)])
    vp = jnp.pad(v_cat, [(0, 0), (0, Lp - L), (0, 0), (0, 0)])
    slab = (jnp.arange(nb) * QB)[:, None] + jnp.arange(QB + wb)[None, :]
    ks = kp[:, slab].astype(jnp.float32)
    vs = vp[:, slab].astype(jnp.float32)
    s = jnp.einsum('bnqkgd,bnmkd->bnkgqm', qp.astype(jnp.float32), ks) * (SW_HD ** -0.5)
    qpos = pos0 + (jnp.arange(nb) * QB)[:, None] + jnp.arange(QB)[None, :]
    kpos = pos0 - wb + slab
    rel = qpos[:, :, None] - kpos[:, None, :]
    allowed = (rel >= 0) & (rel < WINDOW) & (kpos[:, None, :] >= 0)
    s = jnp.where(allowed[None, :, None, None], s, NEG)
    sink = sinks.astype(jnp.float32).reshape(SW_HKV, G)[None, None, :, :, None]
    m = jnp.maximum(jnp.max(s, axis=-1), sink)
    p = jnp.exp(s - m[..., None])
    den = jnp.sum(p, axis=-1) + jnp.exp(sink - m)
    o = jnp.einsum('bnkgqm,bnmkd->bnqkgd', p, vs) / jnp.moveaxis(den, -1, 2)[..., None]
    return o.reshape(B, Lp, SW_QW)[:, :L].astype(q.dtype)


def token_mixers(h, p, conv_past, s0, k_past, v_past, pos0):
    B, L, _ = h.shape
    wb = k_past.shape[1]
    proj = h @ p['w_in']
    qkv_pre, z, b_log, a_in, sq, sk, sv, gate_a, gate_b = split_cols(proj, IN_SPLITS)
    xcat = jnp.concatenate([conv_past.astype(h.dtype), qkv_pre], axis=1)
    new_conv = xcat[:, -(DN_CONV - 1):]
    conv = lax.conv_general_dilated(xcat, p['conv_w'][:, None, :].astype(h.dtype), (1,), 'VALID',
                                    dimension_numbers=('NWC', 'WIO', 'NWC'), feature_group_count=CONV_CH)
    qkv = jax.nn.silu(conv)
    dq, dk, dv = split_cols(qkv, (DN_QW, DN_QW, DN_VW))
    rep = DN_HV // DN_HK
    dq = jnp.repeat(l2norm(dq.reshape(B, L, DN_HK, DN_DK)), rep, axis=2) * (DN_DK ** -0.5)
    dk = jnp.repeat(l2norm(dk.reshape(B, L, DN_HK, DN_DK)), rep, axis=2)
    dv = dv.reshape(B, L, DN_HV, DN_DV).astype(jnp.float32)
    beta = jax.nn.sigmoid(b_log.astype(jnp.float32))
    g = -jnp.exp(p['a_log'].astype(jnp.float32)) * jax.nn.softplus(
        a_in.astype(jnp.float32) + p['dt_bias'].astype(jnp.float32))
    o_dn, s_fin = gated_delta_chunked(dq, dk, dv, g, beta, s0.astype(jnp.float32))
    o_dn = rmsnorm(o_dn, p['dn_norm_g']) * jax.nn.silu(z.reshape(B, L, DN_HV, DN_DV).astype(jnp.float32))
    y_a = o_dn.reshape(B, L, DN_VW).astype(h.dtype) @ p['w_branch_a']
    pos = pos0 + jnp.arange(L, dtype=jnp.float32)
    sq = rotary(rmsnorm(sq.reshape(B, L, SW_HQ, SW_HD), p['qn_g']), pos)
    sk = rotary(rmsnorm(sk.reshape(B, L, SW_HKV, SW_HD), p['kn_g']), pos)
    sv = sv.reshape(B, L, SW_HKV, SW_HD)
    k_cat = jnp.concatenate([k_past.astype(h.dtype), sk], axis=1)
    v_cat = jnp.concatenate([v_past.astype(h.dtype), sv], axis=1)
    y_b = swa_banded(sq, k_cat, v_cat, p['sinks'], pos0, wb) @ p['w_branch_b']
    merged = jax.nn.sigmoid(gate_a) * y_a + jax.nn.sigmoid(gate_b) * y_b
    out = merged @ p['w_out']
    return out, new_conv, s_fin.astype(s0.dtype), k_cat[:, -wb:], v_cat[:, -wb:]


def peer(h, p):
    B, L, D = h.shape
    T = B * L
    TB = min(PEER_TBLOCK, T)
    nb = -(-T // TB)
    hf = jnp.pad(h.reshape(T, D), [(0, nb * TB - T), (0, 0)]).reshape(nb, TB, D)

    def block(hb):
        q = (hb @ p['peer_wq']).reshape(TB, PEER_HEADS, 2, PEER_DKEY // 2).astype(jnp.float32)
        sc = jnp.einsum('thsd,hsnd->thsn', q, p['peer_subkeys'].astype(jnp.float32))
        top_s, top_i = lax.top_k(sc, PEER_TOPK)
        cand = (top_s[:, :, 0, :, None] + top_s[:, :, 1, None, :]).reshape(TB, PEER_HEADS, -1)
        cidx = (top_i[:, :, 0, :, None] * PEER_NKEYS + top_i[:, :, 1, None, :]).reshape(TB, PEER_HEADS, -1)
        best, sel = lax.top_k(cand, PEER_TOPK)
        eidx = jnp.take_along_axis(cidx, sel, axis=-1).reshape(TB, -1)
        gate = jax.nn.softmax(best, axis=-1).reshape(TB, -1)
        ue = p['peer_u'][eidx]
        act = jax.nn.gelu(jnp.einsum('td,ted->te', hb, ue).astype(jnp.float32))
        ve = p['peer_v'][eidx]
        return jnp.einsum('te,ted->td', (gate * act).astype(hb.dtype), ve)

    out = lax.map(block, hf).reshape(nb * TB, D)[:T]
    return out.reshape(B, L, D)


def trunk_layer(x, c, p, conv_past, s0, k_past, v_past, pos0):
    mod = jax.nn.silu(c) @ p['w_mod'] + p['b_mod']
    sh1, sc1, g1, sh2, sc2, g2 = jnp.split(mod[:, None, :], 6, axis=-1)
    h = rmsnorm(x, p['norm1_g']) * (1 + sc1) + sh1
    mix, new_conv, new_s, new_k, new_v = token_mixers(h, p, conv_past, s0, k_past, v_past, pos0)
    x = x + g1 * mix
    h2 = rmsnorm(x, p['norm2_g']) * (1 + sc2) + sh2
    x = x + g2 * peer(h2, p)
    return x, new_conv, new_s, new_k, new_v


def setup_inputs(seed: int = 0) -> dict:
    key = jax.random.key(seed)
    ks = jax.random.split(key, 28)
    f32 = jnp.float32
    wb = min(WINDOW, PAST_LEN)

    def nrm(k, shape, scale):
        return jax.random.normal(k, shape, f32) * scale

    def gain(k, shape):
        return 1.0 + 0.02 * jax.random.normal(k, shape, f32)

    dt = jnp.exp(jax.random.uniform(ks[14], (DEPTH, DN_HV), f32, math.log(1e-3), math.log(1e-1)))
    dt_bias = dt + jnp.log(-jnp.expm1(-dt))
    a_log = jnp.log(jax.random.uniform(ks[13], (DEPTH, DN_HV), f32, 1.0, 16.0))
    return {
        'x_prompt': nrm(ks[0], (BATCH, SEQ, D_MODEL), 1.0),
        'x_sample': nrm(ks[1], (DEC_BATCH, DEC_SEQ, D_MODEL), 1.0),
        'state_conv': nrm(ks[2], (DEPTH, DEC_BATCH, DN_CONV - 1, CONV_CH), 1.0),
        'state_delta': nrm(ks[3], (DEPTH, DEC_BATCH, DN_HV, DN_DK, DN_DV), 0.1),
        'cache_swa_k': nrm(ks[4], (DEPTH, DEC_BATCH, wb, SW_HKV, SW_HD), 1.0),
        'cache_swa_v': nrm(ks[5], (DEPTH, DEC_BATCH, wb, SW_HKV, SW_HD), 1.0),
        'c_prompt': nrm(ks[6], (BATCH, D_MODEL), 1.0),
        'c_sample': nrm(ks[7], (DEC_BATCH, D_MODEL), 1.0),
        'norm1_g': gain(ks[8], (DEPTH, D_MODEL)),
        'norm2_g': gain(ks[9], (DEPTH, D_MODEL)),
        'w_mod': nrm(ks[10], (DEPTH, D_MODEL, 6 * D_MODEL), 0.5 * D_MODEL ** -0.5),
        'b_mod': nrm(ks[11], (DEPTH, 6 * D_MODEL), 0.02),
        'w_in': nrm(ks[12], (DEPTH, D_MODEL, IN_COLS), D_MODEL ** -0.5),
        'conv_w': nrm(ks[15], (DEPTH, DN_CONV, CONV_CH), DN_CONV ** -0.5),
        'a_log': a_log,
        'dt_bias': dt_bias,
        'dn_norm_g': gain(ks[16], (DEPTH, DN_DV)),
        'qn_g': gain(ks[17], (DEPTH, SW_HD)),
        'kn_g': gain(ks[18], (DEPTH, SW_HD)),
        'sinks': nrm(ks[19], (DEPTH, SW_HQ), 0.5),
        'w_branch_a': nrm(ks[20], (DEPTH, DN_VW, D_MODEL), DN_VW ** -0.5),
        'w_branch_b': nrm(ks[21], (DEPTH, SW_QW, D_MODEL), SW_QW ** -0.5),
        'w_out': nrm(ks[22], (DEPTH, D_MODEL, D_MODEL), D_MODEL ** -0.5),
        'peer_wq': nrm(ks[23], (DEPTH, D_MODEL, PEER_HEADS * PEER_DKEY), D_MODEL ** -0.5),
        'peer_subkeys': nrm(ks[24], (DEPTH, PEER_HEADS, 2, PEER_NKEYS, PEER_DKEY // 2), (PEER_DKEY // 2) ** -0.5),
        'peer_u': nrm(ks[25], (DEPTH, PEER_EXPERTS, D_MODEL), D_MODEL ** -0.5),
        'peer_v': nrm(ks[26], (DEPTH, PEER_EXPERTS, D_MODEL), 1.0),
    }


def reference(x_prompt, x_sample, state_conv, state_delta, cache_swa_k, cache_swa_v, c_prompt, c_sample,
              norm1_g, norm2_g, w_mod, b_mod, w_in, conv_w, a_log, dt_bias, dn_norm_g, qn_g, kn_g, sinks,
              w_branch_a, w_branch_b, w_out, peer_wq, peer_subkeys, peer_u, peer_v):
    wb = min(WINDOW, PAST_LEN)
    bp = x_prompt.shape[0]
    dtype = x_prompt.dtype
    zero_conv = jnp.zeros((bp, DN_CONV - 1, CONV_CH), dtype)
    zero_delta = jnp.zeros((bp, DN_HV, DN_DK, DN_DV), dtype)
    zero_kv = jnp.zeros((bp, wb, SW_HKV, SW_HD), dtype)
    yp, ys = x_prompt, x_sample
    pc_l, pd_l, pk_l, pv_l = [], [], [], []
    sc_l, sd_l, sk_l, sv_l = [], [], [], []
    for l in range(DEPTH):
        p = dict(norm1_g=norm1_g[l], norm2_g=norm2_g[l], w_mod=w_mod[l], b_mod=b_mod[l], w_in=w_in[l],
                 conv_w=conv_w[l], a_log=a_log[l], dt_bias=dt_bias[l], dn_norm_g=dn_norm_g[l],
                 qn_g=qn_g[l], kn_g=kn_g[l], sinks=sinks[l], w_branch_a=w_branch_a[l],
                 w_branch_b=w_branch_b[l], w_out=w_out[l], peer_wq=peer_wq[l],
                 peer_subkeys=peer_subkeys[l], peer_u=peer_u[l], peer_v=peer_v[l])
        yp, pc, pd, pk, pv = trunk_layer(yp, c_prompt, p, zero_conv, zero_delta, zero_kv, zero_kv, 0)
        ys, sc, sd, sk, sv = trunk_layer(ys, c_sample, p, state_conv[l], state_delta[l],
                                         cache_swa_k[l], cache_swa_v[l], PAST_LEN)
        pc_l.append(pc); pd_l.append(pd); pk_l.append(pk); pv_l.append(pv)
        sc_l.append(sc); sd_l.append(sd); sk_l.append(sk); sv_l.append(sv)
    return (yp, ys, jnp.stack(pc_l), jnp.stack(pd_l), jnp.stack(pk_l), jnp.stack(pv_l),
            jnp.stack(sc_l), jnp.stack(sd_l), jnp.stack(sk_l), jnp.stack(sv_l))
```

```python
import functools
import math

import numpy as np
import jax
import jax.numpy as jnp
from jax import lax
from jax.experimental import pallas as pl
from jax.experimental.pallas import tpu as pltpu

F32 = jnp.float32
BF16 = jnp.bfloat16
HIGHEST = lax.Precision.HIGHEST

D_MODEL = 2048
DN_HK, DN_HV, DN_DK, DN_DV = 16, 32, 128, 128
DN_CONV = 4
DN_CHUNK = 64
SW_HQ, SW_HKV, SW_HD = 32, 8, 64
WINDOW = 128
ROT_DIM = SW_HD // 4
ROPE_THETA = 500000.0
PEER_HEADS, PEER_NKEYS, PEER_TOPK = 8, 128, 16
PEER_EXPERTS = PEER_NKEYS * PEER_NKEYS
PAST_LEN = 16384
EPS = 1e-6
NEG = -1e30

DN_QW = DN_HK * DN_DK
DN_VW = DN_HV * DN_DV
CONV_CH = 2 * DN_QW + DN_VW
SW_QW = SW_HQ * SW_HD
SW_KW = SW_HKV * SW_HD
MAIN_COLS = CONV_CH + DN_VW
BA_COLS = 2 * DN_HV
TAIL_COLS = SW_QW + 2 * SW_KW + 2 * D_MODEL

LANE = 128
VMEM_LIMIT = 56 * 1024 * 1024


def _cparams(sem):
    return pltpu.CompilerParams(dimension_semantics=sem, vmem_limit_bytes=VMEM_LIMIT)


def _dot(a, b, precision=None):
    return jnp.dot(a, b, preferred_element_type=F32, precision=precision)


def _dot_nt(a, b, precision=None):
    return lax.dot_general(a, b, (((1,), (1,)), ((), ())), preferred_element_type=F32, precision=precision)


def _dot_tn(a, b, precision=None):
    return lax.dot_general(a, b, (((0,), (0,)), ((), ())), preferred_element_type=F32, precision=precision)


def _sigmoid(x):
    return 1.0 / (1.0 + jnp.exp(-x))


def _silu(x):
    return x * _sigmoid(x)


def _softplus(x):
    return jnp.maximum(x, 0.0) + jnp.log1p(jnp.exp(-jnp.abs(x)))


def _mod_kernel(c_ref, w_ref, b_ref, o_ref):
    a = _silu(c_ref[...])
    o_ref[...] = _dot(a, w_ref[...], HIGHEST) + b_ref[...]


def modulation(c_all, w_mod, b_mod):
    m, d = c_all.shape
    n = w_mod.shape[1]
    tn = 1024
    return pl.pallas_call(
        _mod_kernel,
        grid=(n // tn,),
        in_specs=[pl.BlockSpec((m, d), lambda j: (0, 0)),
                  pl.BlockSpec((d, tn), lambda j: (0, j)),
                  pl.BlockSpec((1, tn), lambda j: (0, j))],
        out_specs=pl.BlockSpec((m, tn), lambda j: (0, j)),
        out_shape=jax.ShapeDtypeStruct((m, n), F32),
        compiler_params=_cparams(("arbitrary",)),
        name="modulation",
    )(c_all, w_mod, b_mod.reshape(1, n))


def _norm_kernel(x_ref, g_ref, sc_ref, sh_ref, o_ref):
    x = x_ref[0]
    y = x * lax.rsqrt(jnp.mean(x * x, axis=-1, keepdims=True) + EPS)
    y = y * g_ref[...]
    o_ref[0] = (y * (1.0 + sc_ref[0]) + sh_ref[0]).astype(o_ref.dtype)


def norm_modulate(x, g, scale, shift, tl):
    b, l, d = x.shape
    r = scale.shape[1]
    rb = 1 if r == 1 else tl
    mod_map = (lambda i, j: (i, 0, 0)) if r == 1 else (lambda i, j: (i, j, 0))
    return pl.pallas_call(
        _norm_kernel,
        grid=(b, l // tl),
        in_specs=[pl.BlockSpec((1, tl, d), lambda i, j: (i, j, 0)),
                  pl.BlockSpec((1, d), lambda i, j: (0, 0)),
                  pl.BlockSpec((1, rb, d), mod_map),
                  pl.BlockSpec((1, rb, d), mod_map)],
        out_specs=pl.BlockSpec((1, tl, d), lambda i, j: (i, j, 0)),
        out_shape=jax.ShapeDtypeStruct((b, l, d), BF16),
        compiler_params=_cparams(("parallel", "parallel")),
        name="norm_modulate",
    )(x, g.reshape(1, d), scale, shift)


def _mm_kernel(x_ref, w_ref, o_ref, *, precision):
    if precision is None:
        acc = _dot(x_ref[...].astype(BF16), w_ref[...].astype(BF16))
    else:
        acc = _dot(x_ref[...].astype(F32), w_ref[...].astype(F32), precision)
    o_ref[...] = acc.astype(o_ref.dtype)


def _mm_merge_kernel(x_ref, w_ref, ga_ref, gb_ref, ya_ref, o_ref):
    acc = _dot(x_ref[...].astype(BF16), w_ref[...].astype(BF16))
    out = _sigmoid(ga_ref[...]) * ya_ref[...] + _sigmoid(gb_ref[...]) * acc
    o_ref[...] = out.astype(o_ref.dtype)


def _mm_resid_kernel(x_ref, w_ref, r_ref, g_ref, o_ref):
    acc = _dot(x_ref[...].astype(BF16), w_ref[...].astype(BF16))
    o_ref[...] = r_ref[...] + g_ref[0] * acc


def _row_tile(t):
    for tm in (1024, 512, 256, 128):
        if t % tm == 0:
            return tm
    return t


def matmul(x, w, col0, n, *, tn=512, out_dtype=F32, precision=None):
    t, k = x.shape
    tm = _row_tile(t)
    tn = min(tn, n)
    cb = col0 // tn
    assert col0 % tn == 0 and n % tn == 0
    return pl.pallas_call(
        functools.partial(_mm_kernel, precision=precision),
        grid=(t // tm, n // tn),
        in_specs=[pl.BlockSpec((tm, k), lambda i, j: (i, 0)),
                  pl.BlockSpec((k, tn), lambda i, j: (0, cb + j))],
        out_specs=pl.BlockSpec((tm, tn), lambda i, j: (i, j)),
        out_shape=jax.ShapeDtypeStruct((t, n), out_dtype),
        compiler_params=_cparams(("parallel", "arbitrary")),
        name="matmul",
    )(x, w)


def matmul_merge(x, w, gates, ga_col0, gb_col0, ya, *, tn=512):
    t, k = x.shape
    n = w.shape[1]
    tm = _row_tile(t)
    ca, cb = ga_col0 // tn, gb_col0 // tn
    return pl.pallas_call(
        _mm_merge_kernel,
        grid=(t // tm, n // tn),
        in_specs=[pl.BlockSpec((tm, k), lambda i, j: (i, 0)),
                  pl.BlockSpec((k, tn), lambda i, j: (0, j)),
                  pl.BlockSpec((tm, tn), lambda i, j: (i, ca + j)),
                  pl.BlockSpec((tm, tn), lambda i, j: (i, cb + j)),
                  pl.BlockSpec((tm, tn), lambda i, j: (i, j))],
        out_specs=pl.BlockSpec((tm, tn), lambda i, j: (i, j)),
        out_shape=jax.ShapeDtypeStruct((t, n), BF16),
        compiler_params=_cparams(("parallel", "arbitrary")),
        name="matmul_merge",
    )(x, w, gates, gates, ya)


def _gate_spec(gate, rows_per_seq, tm, tn):
    r = gate.shape[1]
    if r == 1:
        assert rows_per_seq % tm == 0
        per = rows_per_seq // tm
        return pl.BlockSpec((1, 1, tn), lambda i, j: (i // per, 0, j))
    assert gate.shape[0] == 1
    return pl.BlockSpec((1, tm, tn), lambda i, j: (0, i, j))


def matmul_resid(x, w, xres, gate, rows_per_seq, *, tn=512):
    t, k = x.shape
    n = w.shape[1]
    tm = _row_tile(t)
    if gate.shape[1] == 1:
        tm = min(tm, rows_per_seq)
    return pl.pallas_call(
        _mm_resid_kernel,
        grid=(t // tm, n // tn),
        in_specs=[pl.BlockSpec((tm, k), lambda i, j: (i, 0)),
                  pl.BlockSpec((k, tn), lambda i, j: (0, j)),
                  pl.BlockSpec((tm, tn), lambda i, j: (i, j)),
                  _gate_spec(gate, rows_per_seq, tm, tn)],
        out_specs=pl.BlockSpec((tm, tn), lambda i, j: (i, j)),
        out_shape=jax.ShapeDtypeStruct((t, n), F32),
        compiler_params=_cparams(("parallel", "arbitrary")),
        name="matmul_resid",
    )(x, w, xres, gate)


def _delta_consts(c, nch):
    r = c * nch
    i = np.arange(r)[:, None]
    j = np.arange(r)[None, :]
    same = (i // c) == (j // c)
    tri = np.stack([same & (i >= j), same & (i > j), i == j, same]).astype(np.float32)
    levels = []
    s = 1
    while s < c:
        levels.append(((i // (2 * s)) == (j // (2 * s))) & ((i // s) % 2 == 1) & ((j // s) % 2 == 0))
        s *= 2
    return jnp.asarray(tri), jnp.asarray(np.stack(levels).astype(np.float32))


def _delta_kernel(pq_ref, pk_ref, pv_ref, z_ref, ba_ref, cpq_ref, cpk_ref, cpv_ref,
                  cwq_ref, cwk_ref, cwv_ref, alog_ref, dtb_ref, ng_ref, s0_ref, tri_ref, lev_ref,
                  o_ref, sfin_ref,
                  xq_buf, xk_buf, xv_buf, ba_buf, s_scr, u_scr, *, l, lpad, c, nch):
    r = c * nch
    hk = pl.program_id(1)
    halo = 8

    for buf, past_ref, x_ref in ((xq_buf, cpq_ref, pq_ref), (xk_buf, cpk_ref, pk_ref), (xv_buf, cpv_ref, pv_ref)):
        buf[0:halo, :] = jnp.zeros((halo, buf.shape[1]), F32)
        buf[halo - (DN_CONV - 1):halo, :] = past_ref[0]
        buf[halo:halo + l, :] = x_ref[0]
        if lpad > l:
            buf[halo + l:halo + lpad, :] = jnp.zeros((lpad - l, buf.shape[1]), F32)
    ba_buf[0:l, :] = ba_ref[0]
    if lpad > l:
        ba_buf[l:lpad, :] = jnp.zeros((lpad - l, LANE), F32)
    s_scr[...] = s0_ref[0]
    u_scr[...] = jnp.zeros_like(u_scr)

    lane = lax.broadcasted_iota(jnp.int32, (1, LANE), 1)
    incl = tri_ref[0]
    strict = tri_ref[1]
    eye = tri_ref[2]
    same = tri_ref[3]
    nlev = lev_ref.shape[0]

    def conv_silu(buf, cw_ref, r0):
        win = buf[pl.ds(r0, r + halo), :]
        acc = None
        for w in range(DN_CONV):
            lo = halo - (DN_CONV - 1) + w
            t = win[lo:lo + r] * cw_ref[w:w + 1, :]
            acc = t if acc is None else acc + t
        return _silu(acc)

    def l2n(x):
        return x * lax.rsqrt(jnp.sum(x * x, axis=-1, keepdims=True) + EPS)

    def group(gi, carry):
        r0 = pl.multiple_of(gi * r, r)
        qn = l2n(conv_silu(xq_buf, cwq_ref, r0)) * (DN_DK ** -0.5)
        kn = l2n(conv_silu(xk_buf, cwk_ref, r0))
        cv = conv_silu(xv_buf, cwv_ref, r0)
        ba = ba_buf[pl.ds(r0, r), :]
        beta_all = _sigmoid(ba)
        g_all = -jnp.exp(alog_ref[...]) * _softplus(ba + dtb_ref[...])
        if lpad > l:
            valid = (r0 + lax.broadcasted_iota(jnp.int32, (r, 1), 0)) < l
            kn = jnp.where(valid, kn, 0.0)
            cv = jnp.where(valid, cv, 0.0)
            beta_all = jnp.where(valid, beta_all, 0.0)
            g_all = jnp.where(valid, g_all, 0.0)
        betas, g2 = [], jnp.zeros((r, LANE), F32)
        for j in range(2):
            betas.append(jnp.sum(jnp.where(lane == 2 * hk + j, beta_all, 0.0), axis=1, keepdims=True))
            gj = jnp.sum(jnp.where(lane == DN_HV + 2 * hk + j, g_all, 0.0), axis=1, keepdims=True)
            g2 = g2 + jnp.where(lane == j, gj, 0.0)
        gcum2 = _dot(incl, g2, HIGHEST)
        gtot2 = _dot(same, g2, HIGHEST)
        gct = gcum2.T
        kn_b = kn.astype(BF16)
        qn_b = qn.astype(BF16)
        kk = _dot_nt(kn_b, kn_b)
        qk = _dot_nt(qn_b, kn_b)
        outs = []
        for j in range(2):
            beta = betas[j]
            gc = gcum2[:, j:j + 1]
            gl = gtot2[:, j:j + 1]
            dec = jnp.exp(jnp.where(incl > 0, gc - gct[j:j + 1, :], NEG))
            nmat = strict * (beta * kk * dec)
            tinv = eye - lev_ref[0] * nmat
            for lv in range(1, nlev):
                t_b = tinv.astype(BF16)
                off = (lev_ref[lv] * nmat).astype(BF16)
                tinv = tinv - _dot(t_b, _dot(off, t_b).astype(BF16))
            vj = cv[:, j * DN_DV:(j + 1) * DN_DV]
            rhs = jnp.concatenate([beta * vj, (beta * jnp.exp(gc)) * kn], axis=1).astype(BF16)
            sol = _dot(tinv.astype(BF16), rhs)
            u0 = sol[:, :DN_DV]
            wmat = sol[:, DN_DV:].astype(BF16)
            qkd = (qk * dec).astype(BF16)
            qg = (qn * jnp.exp(gc)).astype(BF16)
            kt = (kn * jnp.exp(gl - gc)).astype(BF16)
            o_parts = []
            for ci in range(nch):
                rows = slice(ci * c, (ci + 1) * c)
                s_old = s_scr[j]
                s_b = s_old.astype(BF16)
                u = u0[rows] - _dot(wmat[rows], s_b)
                u_b = u.astype(BF16)
                u_scr[rows, :] = u_b
                o_parts.append(_dot(qg[rows], s_b) + _dot(qkd[rows], u_scr[...]))
                s_scr[j] = jnp.exp(gl[ci * c:ci * c + 1, :]) * s_old + _dot_tn(kt[rows], u_b)
            outs.append(jnp.concatenate(o_parts, axis=0) if nch > 1 else o_parts[0])
        nout = min(r, l)
        for j in range(2):
            o = outs[j][:nout]
            y = o * lax.rsqrt(jnp.mean(o * o, axis=-1, keepdims=True) + EPS) * ng_ref[...]
            zj = z_ref[0, pl.ds(r0, nout), j * DN_DV:(j + 1) * DN_DV]
            o_ref[0, pl.ds(r0, nout), j * DN_DV:(j + 1) * DN_DV] = (y * _silu(zj)).astype(o_ref.dtype)
        return carry

    lax.fori_loop(0, lpad // r, group, 0)
    sfin_ref[0] = s_scr[...]


def delta_branch(p_main, p_ba, conv_past, conv_w, alog128, dtb128, ng, s0, *, nch, out_dtype):
    b, l, _ = p_main.shape
    c = DN_CHUNK
    r = c * nch
    lpad = -(-l // r) * r
    tri, lev = _delta_consts(c, nch)
    kq, kv = DN_QW // DN_DK, (2 * DN_QW) // (2 * DN_DV)
    kz = CONV_CH // (2 * DN_DV)
    in_specs = [
        pl.BlockSpec((1, l, DN_DK), lambda i, h: (i, 0, h)),
        pl.BlockSpec((1, l, DN_DK), lambda i, h: (i, 0, kq + h)),
        pl.BlockSpec((1, l, 2 * DN_DV), lambda i, h: (i, 0, kv + h)),
        pl.BlockSpec((1, l, 2 * DN_DV), lambda i, h: (i, 0, kz + h)),
        pl.BlockSpec((1, l, LANE), lambda i, h: (i, 0, 0)),
        pl.BlockSpec((1, DN_CONV - 1, DN_DK), lambda i, h: (i, 0, h)),
        pl.BlockSpec((1, DN_CONV - 1, DN_DK), lambda i, h: (i, 0, kq + h)),
        pl.BlockSpec((1, DN_CONV - 1, 2 * DN_DV), lambda i, h: (i, 0, kv + h)),
        pl.BlockSpec((DN_CONV, DN_DK), lambda i, h: (0, h)),
        pl.BlockSpec((DN_CONV, DN_DK), lambda i, h: (0, kq + h)),
        pl.BlockSpec((DN_CONV, 2 * DN_DV), lambda i, h: (0, kv + h)),
        pl.BlockSpec((1, LANE), lambda i, h: (0, 0)),
        pl.BlockSpec((1, LANE), lambda i, h: (0, 0)),
        pl.BlockSpec((1, DN_DV), lambda i, h: (0, 0)),
        pl.BlockSpec((1, 2, DN_DK, DN_DV), lambda i, h: (i, h, 0, 0)),
        pl.BlockSpec(tri.shape, lambda i, h: (0, 0, 0)),
        pl.BlockSpec(lev.shape, lambda i, h: (0, 0, 0)),
    ]
    out_specs = [pl.BlockSpec((1, l, 2 * DN_DV), lambda i, h: (i, 0, h)),
                 pl.BlockSpec((1, 2, DN_DK, DN_DV), lambda i, h: (i, h, 0, 0))]
    out_shape = [jax.ShapeDtypeStruct((b, l, DN_VW), out_dtype),
                 jax.ShapeDtypeStruct((b, DN_HV, DN_DK, DN_DV), F32)]
    scratch = [pltpu.VMEM((8 + lpad, DN_DK), F32), pltpu.VMEM((8 + lpad, DN_DK), F32),
               pltpu.VMEM((8 + lpad, 2 * DN_DV), F32), pltpu.VMEM((lpad, LANE), F32),
               pltpu.VMEM((2, DN_DK, DN_DV), F32), pltpu.VMEM((r, DN_DV), BF16)]
    return pl.pallas_call(
        functools.partial(_delta_kernel, l=l, lpad=lpad, c=c, nch=nch),
        grid=(b, DN_HK),
        in_specs=in_specs, out_specs=out_specs, out_shape=out_shape, scratch_shapes=scratch,
        compiler_params=_cparams(("parallel", "arbitrary")),
        name="delta_branch",
    )(p_main, p_main, p_main, p_main, p_ba, conv_past, conv_past, conv_past,
      conv_w, conv_w, conv_w, alog128, dtb128, ng, s0, tri, lev)


def _rope_tables(pos):
    half = ROT_DIM // 2
    inv = 1.0 / (ROPE_THETA ** (jnp.arange(0, ROT_DIM, 2, dtype=F32) / ROT_DIM))
    ang = pos[:, None] * inv[None, :]
    cos, sin = jnp.cos(ang), jnp.sin(ang)
    n = pos.shape[0]
    rest = SW_HD - ROT_DIM
    ctab = jnp.concatenate([cos, cos, jnp.ones((n, rest), F32)], axis=1)
    stab = jnp.concatenate([-sin, sin, jnp.zeros((n, rest), F32)], axis=1)
    return ctab, stab


def _rope_perm():
    half = ROT_DIM // 2
    p = np.zeros((SW_HD, SW_HD), np.float32)
    for d in range(half):
        p[d + half, d] = 1.0
        p[d, d + half] = 1.0
    return jnp.asarray(p)


def _swa_kernel(*refs, qb, qbp, pos0, multi):
    if multi:
        (q_ref, kc_ref, vc_ref, kp_ref, vp_ref, kpast_ref, vpast_ref, ct_ref, st_ref, ctp_ref, stp_ref,
         qg_ref, kg_ref, sink_ref, perm_ref, o_ref, knew_ref, qbuf, kbuf, vbuf) = refs
    else:
        (q_ref, kc_ref, vc_ref, kpast_ref, vpast_ref, ct_ref, st_ref,
         qg_ref, kg_ref, sink_ref, perm_ref, o_ref, knew_ref, qbuf, kbuf, vbuf) = refs
    blk = pl.program_id(1)
    g = SW_HQ // SW_HKV
    w = WINDOW
    perm = perm_ref[...]

    if qbp > qb:
        qbuf[...] = jnp.zeros_like(qbuf)
        kbuf[...] = jnp.zeros_like(kbuf)
        vbuf[...] = jnp.zeros_like(vbuf)
    qbuf[0:qb, :] = q_ref[0]
    kbuf[0:qb, :] = kc_ref[0]
    vbuf[0:qb, :] = vc_ref[0]

    ct = ct_ref[...]
    st = st_ref[...]

    def norm_rope(x, gain, c_tab, s_tab):
        y = x * lax.rsqrt(jnp.mean(x * x, axis=-1, keepdims=True) + EPS) * gain
        return y * c_tab + _dot(y, perm, HIGHEST) * s_tab

    row = lax.broadcasted_iota(jnp.int32, (g * qbp, w + qbp), 0) & (qbp - 1)
    col = lax.broadcasted_iota(jnp.int32, (g * qbp, w + qbp), 1)
    kpos_ok = (pos0 + blk * qb - w + col) >= 0
    allowed = (col <= row + w) & (col > row) & kpos_ok

    for kh in range(SW_HKV):
        ksl = slice(kh * SW_HD, (kh + 1) * SW_HD)
        k_cur = norm_rope(kbuf[:, ksl], kg_ref[...], ct, st)
        knew_ref[0, :, ksl] = k_cur[:qb]
        v_cur = vbuf[:, ksl]
        if multi:
            k_prev_blk = norm_rope(kp_ref[0, :, ksl], kg_ref[...], ctp_ref[...], stp_ref[...])
            first = blk == 0
            k_prev = jnp.where(first, kpast_ref[0, :, ksl], k_prev_blk)
            v_prev = jnp.where(first, vpast_ref[0, :, ksl], vp_ref[0, :, ksl])
        else:
            k_prev = kpast_ref[0, :, ksl]
            v_prev = vpast_ref[0, :, ksl]
        kslab = jnp.concatenate([k_prev, k_cur], axis=0).astype(BF16)
        vslab = jnp.concatenate([v_prev, v_cur], axis=0).astype(BF16)
        qs, sinks = [], []
        for gi in range(g):
            hq = kh * g + gi
            qh = norm_rope(qbuf[:, hq * SW_HD:(hq + 1) * SW_HD], qg_ref[...], ct, st)
            qs.append(qh)
            sinks.append(jnp.broadcast_to(sink_ref[0:1, hq:hq + 1], (qbp, 1)))
        qs = jnp.concatenate(qs, axis=0).astype(BF16)
        sink = jnp.concatenate(sinks, axis=0)
        s = _dot_nt(qs, kslab) * (SW_HD ** -0.5)
        s = jnp.where(allowed, s, NEG)
        m = jnp.maximum(jnp.max(s, axis=-1, keepdims=True), sink)
        p = jnp.exp(s - m)
        den = jnp.sum(p, axis=-1, keepdims=True) + jnp.exp(sink - m)
        o = _dot(p.astype(BF16), vslab) / den
        for gi in range(g):
            hq = kh * g + gi
            o_ref[0, :, hq * SW_HD:(hq + 1) * SW_HD] = o[gi * qbp:gi * qbp + qb].astype(o_ref.dtype)


def swa_branch(p_tail, k_past, v_past, qn_g, kn_g, sinks, pos0, *, out_dtype):
    b, l, _ = p_tail.shape
    qb = min(WINDOW, l)
    qbp = -(-qb // 8) * 8
    nb = l // qb
    multi = nb > 1
    assert qbp & (qbp - 1) == 0 and (multi is False or qbp == qb)
    ctab, stab = _rope_tables(pos0 + jnp.arange(l, dtype=F32))
    if qbp > qb:
        ctab = jnp.pad(ctab, ((0, qbp - qb), (0, 0)), constant_values=1.0)
        stab = jnp.pad(stab, ((0, qbp - qb), (0, 0)))
    kcb, vcb = SW_QW // SW_KW, SW_QW // SW_KW + 1
    cur = lambda i, j: (i, j, 0)
    prev = lambda i, j: (i, jnp.maximum(j - 1, 0), 0)
    in_specs = [pl.BlockSpec((1, qb, SW_QW), cur),
                pl.BlockSpec((1, qb, SW_KW), lambda i, j: (i, j, kcb)),
                pl.BlockSpec((1, qb, SW_KW), lambda i, j: (i, j, vcb))]
    args = [p_tail, p_tail, p_tail]
    if multi:
        in_specs += [pl.BlockSpec((1, qb, SW_KW), lambda i, j: (i, jnp.maximum(j - 1, 0), kcb)),
                     pl.BlockSpec((1, qb, SW_KW), lambda i, j: (i, jnp.maximum(j - 1, 0), vcb))]
        args += [p_tail, p_tail]
    in_specs += [pl.BlockSpec((1, WINDOW, SW_KW), lambda i, j: (i, 0, 0)),
                 pl.BlockSpec((1, WINDOW, SW_KW), lambda i, j: (i, 0, 0)),
                 pl.BlockSpec((qbp, SW_HD), lambda i, j: (j, 0)),
                 pl.BlockSpec((qbp, SW_HD), lambda i, j: (j, 0))]
    args += [k_past, v_past, ctab, stab]
    if multi:
        in_specs += [pl.BlockSpec((qb, SW_HD), lambda i, j: (jnp.maximum(j - 1, 0), 0)),
                     pl.BlockSpec((qb, SW_HD), lambda i, j: (jnp.maximum(j - 1, 0), 0))]
        args += [ctab, stab]
    in_specs += [pl.BlockSpec((1, SW_HD), lambda i, j: (0, 0)),
                 pl.BlockSpec((1, SW_HD), lambda i, j: (0, 0)),
                 pl.BlockSpec((1, SW_HQ), lambda i, j: (0, 0)),
                 pl.BlockSpec((SW_HD, SW_HD), lambda i, j: (0, 0))]
    args += [qn_g.reshape(1, SW_HD), kn_g.reshape(1, SW_HD), sinks.reshape(1, SW_HQ), _rope_perm()]
    return pl.pallas_call(
        functools.partial(_swa_kernel, qb=qb, qbp=qbp, pos0=pos0, multi=multi),
        grid=(b, nb),
        in_specs=in_specs,
        out_specs=[pl.BlockSpec((1, qb, SW_QW), cur), pl.BlockSpec((1, qb, SW_KW), cur)],
        out_shape=[jax.ShapeDtypeStruct((b, l, SW_QW), out_dtype), jax.ShapeDtypeStruct((b, l, SW_KW), F32)],
        scratch_shapes=[pltpu.VMEM((qbp, SW_QW), F32), pltpu.VMEM((qbp, SW_KW), F32), pltpu.VMEM((qbp, SW_KW), F32)],
        compiler_params=_cparams(("parallel", "arbitrary")),
        name="swa_branch",
    )(*args)


def _extract_top(work, nrow, k):
    rowid = lax.broadcasted_iota(jnp.int32, work.shape, 0)
    vals = []
    for _ in range(k):
        m = jnp.max(work, axis=0, keepdims=True)
        vals.append(m)
        first = jnp.min(jnp.where(work == m, rowid, nrow), axis=0, keepdims=True)
        work = jnp.where(rowid == first, -jnp.inf, work)
    return vals


def _peer_topk_kernel(q_ref, sk_ref, s0_ref, s1_ref, e1_ref, w0_ref, thr_ref):
    tb = q_ref.shape[0]
    kk = PEER_TOPK
    row16 = lax.broadcasted_iota(jnp.int32, (kk, tb), 0)
    for h in range(PEER_HEADS):
        sc, tops = [], []
        for s in range(2):
            col = (h * 2 + s) * PEER_NKEYS
            sc_t = _dot_nt(sk_ref[h, s], q_ref[:, col:col + PEER_NKEYS], HIGHEST)
            sc.append(sc_t)
            tops.append(_extract_top(sc_t, PEER_NKEYS, kk))
        a0, a1 = tops
        v1 = jnp.zeros((kk, tb), F32)
        for r in range(kk):
            v1 = v1 + jnp.where(row16 == r, a1[r], 0.0)
        cand = jnp.concatenate([jnp.broadcast_to(a0[r], (kk, tb)) + v1 for r in range(kk)], axis=0)
        best = _extract_top(cand, kk * kk, kk)
        mx = best[0]
        den = jnp.zeros_like(mx)
        for r in range(kk):
            den = den + jnp.exp(best[r] - mx)
        s0_ref[h] = sc[0]
        s1_ref[h] = sc[1]
        e1_ref[h] = jnp.exp(sc[1] - a1[0])
        w0_ref[h] = jnp.exp(sc[0] - a0[0]) / den
        thr_ref[h] = best[kk - 1]


def peer_topk(q, subkeys, tb):
    t = q.shape[0]
    big = jax.ShapeDtypeStruct((PEER_HEADS, PEER_NKEYS, t), F32)
    bspec = pl.BlockSpec((PEER_HEADS, PEER_NKEYS, tb), lambda i: (0, 0, i))
    return pl.pallas_call(
        _peer_topk_kernel,
        grid=(t // tb,),
        in_specs=[pl.BlockSpec((tb, q.shape[1]), lambda i: (i, 0)),
                  pl.BlockSpec(subkeys.shape, lambda i: (0, 0, 0, 0))],
        out_specs=[bspec, bspec, bspec, bspec, pl.BlockSpec((PEER_HEADS, 1, tb), lambda i: (0, 0, i))],
        out_shape=[big, big, big, big, jax.ShapeDtypeStruct((PEER_HEADS, 1, t), F32)],
        compiler_params=_cparams(("parallel",)),
        name="peer_topk",
    )(q, subkeys)


def _gelu_tanh(x):
    return 0.5 * x * (1.0 + jnp.tanh(math.sqrt(2.0 / math.pi) * (x + 0.044715 * (x * x * x))))


def _peer_mix_kernel(h_ref, u_ref, v_ref, s0_ref, s1_ref, e1_ref, w0_ref, thr_ref, x_ref, g_ref,
                     o_ref, acc_ref, *, et):
    e = pl.program_id(1)

    @pl.when(e == 0)
    def _():
        acc_ref[...] = jnp.zeros_like(acc_ref)

    act = _gelu_tanh(_dot_nt(u_ref[...].astype(BF16), h_ref[...]))
    parts = []
    for ii in range(et // PEER_NKEYS):
        i = e * (et // PEER_NKEYS) + ii
        gate = None
        for h in range(PEER_HEADS):
            sel = (s1_ref[h] + s0_ref[h, pl.ds(i, 1), :]) >= thr_ref[h]
            term = jnp.where(sel, e1_ref[h] * w0_ref[h, pl.ds(i, 1), :], 0.0)
            gate = term if gate is None else gate + term
        parts.append((gate * act[ii * PEER_NKEYS:(ii + 1) * PEER_NKEYS]).astype(BF16))
    pt = jnp.concatenate(parts, axis=0) if len(parts) > 1 else parts[0]
    acc_ref[...] += _dot_tn(pt, v_ref[...].astype(BF16))

    @pl.when(e == pl.num_programs(1) - 1)
    def _():
        o_ref[...] = x_ref[...] + g_ref[0] * acc_ref[...]


def peer_mix(h2, peer_u, peer_v, s0t, s1t, e1t, w0t, thr, xres, gate, rows_per_seq, *, tb, et):
    t, d = h2.shape
    ne = peer_u.shape[0]
    if gate.shape[1] == 1:
        assert rows_per_seq % tb == 0
        per = rows_per_seq // tb
        gspec = pl.BlockSpec((1, 1, d), lambda i, e: (i // per, 0, 0))
    else:
        gspec = pl.BlockSpec((1, tb, d), lambda i, e: (0, i, 0))
    bspec = pl.BlockSpec((PEER_HEADS, PEER_NKEYS, tb), lambda i, e: (0, 0, i))
    return pl.pallas_call(
        functools.partial(_peer_mix_kernel, et=et),
        grid=(t // tb, ne // et),
        in_specs=[pl.BlockSpec((tb, d), lambda i, e: (i, 0)),
                  pl.BlockSpec((et, d), lambda i, e: (e, 0)),
                  pl.BlockSpec((et, d), lambda i, e: (e, 0)),
                  bspec, bspec, bspec, bspec,
                  pl.BlockSpec((PEER_HEADS, 1, tb), lambda i, e: (0, 0, i)),
                  pl.BlockSpec((tb, d), lambda i, e: (i, 0)),
                  gspec],
        out_specs=pl.BlockSpec((tb, d), lambda i, e: (i, 0)),
        out_shape=jax.ShapeDtypeStruct((t, d), F32),
        scratch_shapes=[pltpu.VMEM((tb, d), F32)],
        compiler_params=_cparams(("parallel", "arbitrary")),
        name="peer_mix",
    )(h2, peer_u, peer_v, s0t, s1t, e1t, w0t, thr, xres, gate)


def _expand_rows(m, reps):
    return jnp.repeat(m, reps, axis=0)[None]


def trunk_layer(x, mods, p, conv_past, s0, k_past, v_past, pos0, *, per_row):
    b, l, d = x.shape
    t = b * l
    sh1, sc1, g1, sh2, sc2, g2 = mods
    if per_row:
        expand = lambda m: _expand_rows(m, l)
        xn, mod_l, tl = x.reshape(1, t, d), t, t
    else:
        expand = lambda m: m[:, None, :]
        xn, mod_l, tl = x, l, min(512, l)
    sh1, sc1, g1, sh2, sc2, g2 = [expand(m) for m in (sh1, sc1, g1, sh2, sc2, g2)]
    seq_dtype = F32 if per_row else BF16

    h = norm_modulate(xn, p['norm1_g'], sc1, sh1, tl).reshape(t, d)
    p_main = matmul(h, p['w_in'], 0, MAIN_COLS)
    p_ba = matmul(h, p['w_ba'], 0, LANE, tn=LANE, precision=HIGHEST)
    p_tail = matmul(h, p['w_tail'], 0, TAIL_COLS)

    o_dn, s_fin = delta_branch(p_main.reshape(b, l, MAIN_COLS), p_ba.reshape(b, l, LANE), conv_past,
                               p['conv_w'], p['alog128'], p['dtb128'], p['dn_norm_g'].reshape(1, DN_DV), s0,
                               nch=2 if per_row else 4, out_dtype=seq_dtype)
    y_a = matmul(o_dn.reshape(t, DN_VW), p['w_branch_a'], 0, d)

    attn, k_new = swa_branch(p_tail.reshape(b, l, TAIL_COLS), k_past, v_past,
                             p['qn_g'], p['kn_g'], p['sinks'], pos0, out_dtype=seq_dtype)
    ga0 = SW_QW + 2 * SW_KW
    merged = matmul_merge(attn.reshape(t, SW_QW), p['w_branch_b'], p_tail, ga0, ga0 + d, y_a)
    x1 = matmul_resid(merged, p['w_out'], x.reshape(t, d), g1, mod_l)

    h2 = norm_modulate(x1.reshape(xn.shape), p['norm2_g'], sc2, sh2, tl).reshape(t, d)
    q = matmul(h2, p['peer_wq'], 0, d)
    tbp = 128 if t <= 128 else 256
    s0t, s1t, e1t, w0t, thr = peer_topk(q, p['peer_subkeys'], tbp)
    tb = min(512, t)
    x2 = peer_mix(h2, p['peer_u'], p['peer_v'], s0t, s1t, e1t, w0t, thr, x1, g2, mod_l, tb=tb, et=256)

    new_conv = p_main.reshape(b, l, MAIN_COLS)[:, :, :CONV_CH]
    sv = p_tail.reshape(b, l, TAIL_COLS)[:, :, SW_QW + SW_KW:SW_QW + 2 * SW_KW]
    return x2.reshape(b, l, d), new_conv, s_fin, k_new, sv


def kernel(x_prompt, x_sample, state_conv, state_delta, cache_swa_k, cache_swa_v, c_prompt, c_sample,
           norm1_g, norm2_g, w_mod, b_mod, w_in, conv_w, a_log, dt_bias, dn_norm_g, qn_g, kn_g, sinks,
           w_branch_a, w_branch_b, w_out, peer_wq, peer_subkeys, peer_u, peer_v):
    depth = w_in.shape[0]
    bp, lp, d = x_prompt.shape
    bs, ls, _ = x_sample.shape
    wb = cache_swa_k.shape[2]
    past_len = PAST_LEN
    assert wb == WINDOW

    c_all = jnp.concatenate([c_prompt, c_sample], axis=0)
    mpad = -(-c_all.shape[0] // 8) * 8
    c_all = jnp.pad(c_all, ((0, mpad - c_all.shape[0]), (0, 0)))
    zero_conv = jnp.zeros((bp, DN_CONV - 1, CONV_CH), F32)
    zero_delta = jnp.zeros((bp, DN_HV, DN_DK, DN_DV), F32)
    zero_kv = jnp.zeros((bp, wb, SW_KW), F32)

    yp, ys = x_prompt, x_sample
    outs = [[] for _ in range(8)]
    for l in range(depth):
        mod = modulation(c_all, w_mod[l], b_mod[l])
        mods_p = jnp.split(mod[:bp], 6, axis=-1)
        mods_s = jnp.split(mod[bp:bp + bs], 6, axis=-1)
        pad = jnp.zeros((LANE - 2 * DN_HV,), F32)
        p = dict(norm1_g=norm1_g[l], norm2_g=norm2_g[l], w_in=w_in[l],
                 w_ba=jnp.pad(w_in[l][:, MAIN_COLS:MAIN_COLS + BA_COLS], ((0, 0), (0, LANE - BA_COLS))),
                 w_tail=w_in[l][:, MAIN_COLS + BA_COLS:],
                 conv_w=conv_w[l],
                 alog128=jnp.concatenate([jnp.zeros((DN_HV,), F32), a_log[l], pad]).reshape(1, LANE),
                 dtb128=jnp.concatenate([jnp.zeros((DN_HV,), F32), dt_bias[l], pad]).reshape(1, LANE),
                 dn_norm_g=dn_norm_g[l], qn_g=qn_g[l], kn_g=kn_g[l], sinks=sinks[l],
                 w_branch_a=w_branch_a[l], w_branch_b=w_branch_b[l], w_out=w_out[l], peer_wq=peer_wq[l],
                 peer_subkeys=peer_subkeys[l], peer_u=peer_u[l], peer_v=peer_v[l])

        yp, pre_p, pd, pk, pv = trunk_layer(yp, mods_p, p, zero_conv, zero_delta, zero_kv, zero_kv, 0,
                                            per_row=False)
        ys, pre_s, sd, sk, sv = trunk_layer(ys, mods_s, p, state_conv[l], state_delta[l],
                                            cache_swa_k[l].reshape(bs, wb, SW_KW),
                                            cache_swa_v[l].reshape(bs, wb, SW_KW), past_len, per_row=True)
        keep = DN_CONV - 1
        outs[0].append(pre_p[:, lp - keep:])
        outs[1].append(pd)
        outs[2].append(pk[:, lp - wb:].reshape(bp, wb, SW_HKV, SW_HD))
        outs[3].append(pv[:, lp - wb:].reshape(bp, wb, SW_HKV, SW_HD))
        outs[4].append(jnp.concatenate([state_conv[l], pre_s], axis=1)[:, -keep:])
        outs[5].append(sd)
        outs[6].append(jnp.concatenate([cache_swa_k[l], sk.reshape(bs, ls, SW_HKV, SW_HD)], axis=1)[:, -wb:])
        outs[7].append(jnp.concatenate([cache_swa_v[l], sv.reshape(bs, ls, SW_HKV, SW_HD)], axis=1)[:, -wb:])
    return (yp, ys) + tuple(jnp.stack(o) for o in outs)
```

```python
import functools
import math

import numpy as np
import jax
import jax.numpy as jnp
from jax import lax
from jax.experimental import pallas as pl
from jax.experimental.pallas import tpu as pltpu

F32 = jnp.float32
BF16 = jnp.bfloat16
HIGHEST = lax.Precision.HIGHEST

D_MODEL = 2048
DN_HK, DN_HV, DN_DK, DN_DV = 16, 32, 128, 128
DN_CONV = 4
DN_CHUNK = 64
SW_HQ, SW_HKV, SW_HD = 32, 8, 64
WINDOW = 128
ROT_DIM = SW_HD // 4
ROPE_THETA = 500000.0
PEER_HEADS, PEER_NKEYS, PEER_TOPK = 8, 128, 16
PEER_EXPERTS = PEER_NKEYS * PEER_NKEYS
PAST_LEN = 16384
EPS = 1e-6
NEG = -1e30

DN_QW = DN_HK * DN_DK
DN_VW = DN_HV * DN_DV
CONV_CH = 2 * DN_QW + DN_VW
SW_QW = SW_HQ * SW_HD
SW_KW = SW_HKV * SW_HD
MAIN_COLS = CONV_CH + DN_VW
BA_COLS = 2 * DN_HV
TAIL_COLS = SW_QW + 2 * SW_KW + 2 * D_MODEL

LANE = 128
SUBLANE = 8
VMEM_LIMIT = 56 * 1024 * 1024
DELTA_HEADS_PER_STEP = 4
PEER_MIX_PIECES = 4


def _cparams(sem):
    return pltpu.CompilerParams(dimension_semantics=sem, vmem_limit_bytes=VMEM_LIMIT)


def _dot(a, b, precision=None):
    return jnp.dot(a, b, preferred_element_type=F32, precision=precision)


def _dot_nt(a, b, precision=None):
    return lax.dot_general(a, b, (((1,), (1,)), ((), ())), preferred_element_type=F32, precision=precision)


def _dot_tn(a, b, precision=None):
    return lax.dot_general(a, b, (((0,), (0,)), ((), ())), preferred_element_type=F32, precision=precision)


def _sigmoid(x):
    return 1.0 / (1.0 + jnp.exp(-x))


def _silu(x):
    return x * _sigmoid(x)


def _softplus(x):
    return jnp.maximum(x, 0.0) + jnp.log1p(jnp.exp(-jnp.abs(x)))


def _mod_kernel(c_ref, w_ref, b_ref, o_ref):
    a = _silu(c_ref[...])
    o_ref[...] = _dot(a, w_ref[...], HIGHEST) + b_ref[...]


def modulation(c_all, w_mod, b_mod, layer):
    m, d = c_all.shape
    n = w_mod.shape[2]
    tn = 1024
    return pl.pallas_call(
        _mod_kernel,
        grid=(n // tn,),
        in_specs=[pl.BlockSpec((m, d), lambda j: (0, 0)),
                  pl.BlockSpec((None, d, tn), lambda j: (layer, 0, j)),
                  pl.BlockSpec((None, 1, tn), lambda j: (layer, 0, j))],
        out_specs=pl.BlockSpec((m, tn), lambda j: (0, j)),
        out_shape=jax.ShapeDtypeStruct((m, n), F32),
        compiler_params=_cparams(("arbitrary",)),
        name="modulation",
    )(c_all, w_mod, b_mod.reshape(b_mod.shape[0], 1, n))


def _norm_kernel(x_ref, g_ref, sc_ref, sh_ref, o_ref):
    x = x_ref[0]
    y = x * lax.rsqrt(jnp.mean(x * x, axis=-1, keepdims=True) + EPS)
    y = y * g_ref[...]
    o_ref[0] = (y * (1.0 + sc_ref[0]) + sh_ref[0]).astype(o_ref.dtype)


def norm_modulate(x, g, scale, shift, tl):
    b, l, d = x.shape
    r = scale.shape[1]
    rb = 1 if r == 1 else tl
    mod_map = (lambda i, j: (i, 0, 0)) if r == 1 else (lambda i, j: (i, j, 0))
    return pl.pallas_call(
        _norm_kernel,
        grid=(b, l // tl),
        in_specs=[pl.BlockSpec((1, tl, d), lambda i, j: (i, j, 0)),
                  pl.BlockSpec((1, d), lambda i, j: (0, 0)),
                  pl.BlockSpec((1, rb, d), mod_map),
                  pl.BlockSpec((1, rb, d), mod_map)],
        out_specs=pl.BlockSpec((1, tl, d), lambda i, j: (i, j, 0)),
        out_shape=jax.ShapeDtypeStruct((b, l, d), BF16),
        compiler_params=_cparams(("parallel", "parallel")),
        name="norm_modulate",
    )(x, g.reshape(1, d), scale, shift)


def _mm_kernel(x_ref, w_ref, o_ref, *, precision):
    if precision is None:
        acc = _dot(x_ref[...].astype(BF16), w_ref[...].astype(BF16))
    else:
        acc = _dot(x_ref[...].astype(F32), w_ref[...].astype(F32), precision)
    o_ref[...] = acc.astype(o_ref.dtype)


def _mm_merge_kernel(x_ref, w_ref, ga_ref, gb_ref, ya_ref, o_ref):
    acc = _dot(x_ref[...].astype(BF16), w_ref[...].astype(BF16))
    out = _sigmoid(ga_ref[...]) * ya_ref[...] + _sigmoid(gb_ref[...]) * acc
    o_ref[...] = out.astype(o_ref.dtype)


def _mm_resid_kernel(x_ref, w_ref, r_ref, g_ref, o_ref):
    acc = _dot(x_ref[...].astype(BF16), w_ref[...].astype(BF16))
    o_ref[...] = r_ref[...] + g_ref[0] * acc


def _row_tile(t, cap=1024):
    for tm in (2048, 1024, 512, 256, 128):
        if tm <= cap and t % tm == 0:
            return tm
    return t


def matmul(x, w, layer, col0, n, *, tn=512, out_dtype=F32, precision=None):
    t, k = x.shape
    tm = _row_tile(t, 2048 if k <= 2048 else 1024)
    tn = min(tn, n)
    cb = col0 // tn
    assert col0 % tn == 0 and n % tn == 0
    return pl.pallas_call(
        functools.partial(_mm_kernel, precision=precision),
        grid=(t // tm, n // tn),
        in_specs=[pl.BlockSpec((tm, k), lambda i, j: (i, 0)),
                  pl.BlockSpec((None, k, tn), lambda i, j: (layer, 0, cb + j))],
        out_specs=pl.BlockSpec((tm, tn), lambda i, j: (i, j)),
        out_shape=jax.ShapeDtypeStruct((t, n), out_dtype),
        compiler_params=_cparams(("parallel", "arbitrary")),
        name="matmul",
    )(x, w)


def matmul_merge(x, w, layer, gates, ga_col0, gb_col0, ya, *, tn=512):
    t, k = x.shape
    n = w.shape[2]
    tm = _row_tile(t)
    ca, cb = ga_col0 // tn, gb_col0 // tn
    return pl.pallas_call(
        _mm_merge_kernel,
        grid=(t // tm, n // tn),
        in_specs=[pl.BlockSpec((tm, k), lambda i, j: (i, 0)),
                  pl.BlockSpec((None, k, tn), lambda i, j: (layer, 0, j)),
                  pl.BlockSpec((tm, tn), lambda i, j: (i, ca + j)),
                  pl.BlockSpec((tm, tn), lambda i, j: (i, cb + j)),
                  pl.BlockSpec((tm, tn), lambda i, j: (i, j))],
        out_specs=pl.BlockSpec((tm, tn), lambda i, j: (i, j)),
        out_shape=jax.ShapeDtypeStruct((t, n), BF16),
        compiler_params=_cparams(("parallel", "arbitrary")),
        name="matmul_merge",
    )(x, w, gates, gates, ya)


def _gate_spec(gate, rows_per_seq, tm, tn):
    r = gate.shape[1]
    if r == 1:
        assert rows_per_seq % tm == 0
        per = rows_per_seq // tm
        return pl.BlockSpec((1, 1, tn), lambda i, j: (i // per, 0, j))
    assert gate.shape[0] == 1
    return pl.BlockSpec((1, tm, tn), lambda i, j: (0, i, j))


def matmul_resid(x, w, layer, xres, gate, rows_per_seq, *, tn=512):
    t, k = x.shape
    n = w.shape[2]
    tm = _row_tile(t)
    if gate.shape[1] == 1:
        tm = min(tm, rows_per_seq)
    return pl.pallas_call(
        _mm_resid_kernel,
        grid=(t // tm, n // tn),
        in_specs=[pl.BlockSpec((tm, k), lambda i, j: (i, 0)),
                  pl.BlockSpec((None, k, tn), lambda i, j: (layer, 0, j)),
                  pl.BlockSpec((tm, tn), lambda i, j: (i, j)),
                  _gate_spec(gate, rows_per_seq, tm, tn)],
        out_specs=pl.BlockSpec((tm, tn), lambda i, j: (i, j)),
        out_shape=jax.ShapeDtypeStruct((t, n), F32),
        compiler_params=_cparams(("parallel", "arbitrary")),
        name="matmul_resid",
    )(x, w, xres, gate)


def _delta_consts(c, nch):
    r = c * nch
    i = np.arange(r)[:, None]
    j = np.arange(r)[None, :]
    same = (i // c) == (j // c)
    tri = np.stack([same & (i >= j), same & (i > j), i == j, same]).astype(np.float32)
    levels = []
    s = 1
    while s < c:
        levels.append(((i // (2 * s)) == (j // (2 * s))) & ((i // s) % 2 == 1) & ((j // s) % 2 == 0))
        s *= 2
    return jnp.asarray(tri), jnp.asarray(np.stack(levels).astype(np.float32)).astype(BF16)


def _delta_kernel(pq_ref, pk_ref, pv_ref, z_ref, ba_ref, cpq_ref, cpk_ref, cpv_ref,
                  cwq_ref, cwk_ref, cwv_ref, alog_ref, dtb_ref, ng_ref, s0_ref, tri_ref, lev_ref,
                  o_ref, sfin_ref,
                  xq_buf, xk_buf, xv_buf, ba_buf, s_scr, u_scr, *, l, c, nch, hb):
    r = c * nch
    rb = min(r, l)
    hblk = pl.program_id(1)
    grp = pl.program_id(2)
    halo = SUBLANE
    keep = DN_CONV - 1
    bufs = ((xq_buf, cpq_ref, pq_ref), (xk_buf, cpk_ref, pk_ref), (xv_buf, cpv_ref, pv_ref))

    @pl.when(grp == 0)
    def _():
        for buf, past_ref, _ in bufs:
            buf[0:halo, :] = jnp.zeros((halo, buf.shape[1]), F32)
            buf[halo - keep:halo, :] = past_ref[0]
        s_scr[...] = s0_ref[0]
        u_scr[...] = jnp.zeros_like(u_scr)

    for buf, _, x_ref in bufs:
        buf[halo:halo + rb, :] = x_ref[0]
        if rb < r:
            buf[halo + rb:halo + r, :] = jnp.zeros((r - rb, buf.shape[1]), F32)
    if rb < r:
        ba_buf[0:rb, :] = ba_ref[0]
        ba_buf[rb:r, :] = jnp.zeros((r - rb, LANE), F32)
        ba = ba_buf[...]
    else:
        ba = ba_ref[0]

    lane = lax.broadcasted_iota(jnp.int32, (1, LANE), 1)
    incl = tri_ref[0]
    strict = tri_ref[1]
    eye = tri_ref[2]
    same = tri_ref[3]
    lcap = min(l, c)
    nlev = math.ceil(math.log2(lcap)) if lcap > 1 else 0
    nch_eff = min(nch, -(-rb // c))

    def conv_silu(buf, cw_ref):
        win = buf[...]
        acc = None
        for w in range(DN_CONV):
            lo = halo - keep + w
            t = win[lo:lo + r] * cw_ref[w:w + 1, :]
            acc = t if acc is None else acc + t
        return _silu(acc)

    def l2n(x):
        return x * lax.rsqrt(jnp.sum(x * x, axis=-1, keepdims=True) + EPS)

    cq = conv_silu(xq_buf, cwq_ref)
    ck = conv_silu(xk_buf, cwk_ref)
    cv = conv_silu(xv_buf, cwv_ref)
    for buf, _, _ in bufs:
        buf[halo - keep:halo, :] = buf[halo + r - keep:halo + r, :]

    beta_all = _sigmoid(ba)
    g_all = -jnp.exp(alog_ref[...]) * _softplus(ba + dtb_ref[...])
    valid = None
    if rb < r:
        valid = lax.broadcasted_iota(jnp.int32, (r, 1), 0) < rb
        beta_all = jnp.where(valid, beta_all, 0.0)
        g_all = jnp.where(valid, g_all, 0.0)
        cv = jnp.where(valid, cv, 0.0)

    betas, g2 = [], jnp.zeros((r, LANE), F32)
    for hv in range(2 * hb):
        head = 2 * hb * hblk + hv
        betas.append(jnp.sum(jnp.where(lane == head, beta_all, 0.0), axis=1, keepdims=True))
        gj = jnp.sum(jnp.where(lane == DN_HV + head, g_all, 0.0), axis=1, keepdims=True)
        g2 = g2 + jnp.where(lane == hv, gj, 0.0)
    gcum2 = _dot(incl, g2, HIGHEST)
    gtot2 = _dot(same, g2, HIGHEST)
    gct = gcum2.T

    nv = 2 * hb
    qn_l, kn_l = [], []
    for kh in range(hb):
        ksl = slice(kh * DN_DK, (kh + 1) * DN_DK)
        qn_l.append(l2n(cq[:, ksl]) * (DN_DK ** -0.5))
        kn = l2n(ck[:, ksl])
        kn_l.append(jnp.where(valid, kn, 0.0) if valid is not None else kn)
    knb_l = [kn.astype(BF16) for kn in kn_l]
    qnb_l = [qn.astype(BF16) for qn in qn_l]
    kk_l = [_dot_nt(knb_l[kh], knb_l[kh]) for kh in range(hb)]
    qk_l = [_dot_nt(qnb_l[kh], knb_l[kh]) for kh in range(hb)]
    gc_l = [gcum2[:, hv:hv + 1] for hv in range(nv)]
    gl_l = [gtot2[:, hv:hv + 1] for hv in range(nv)]
    dec_l = [jnp.exp(jnp.where(incl > 0, gc_l[hv] - gct[hv:hv + 1, :], NEG)) for hv in range(nv)]
    nmat_l = [strict * (betas[hv] * kk_l[hv // 2] * dec_l[hv]) for hv in range(nv)]
    nb_l = [nmat.astype(BF16) for nmat in nmat_l]
    if nlev >= 1:
        tinv_l = [eye - lev_ref[0].astype(F32) * nmat for nmat in nmat_l]
    else:
        tinv_l = [eye] * nv
    for lv in range(1, nlev):
        tb_l = [tinv.astype(BF16) for tinv in tinv_l]
        in_l = [_dot(lev_ref[lv] * nb_l[hv], tb_l[hv]).astype(BF16) for hv in range(nv)]
        tinv_l = [tinv_l[hv] - _dot(tb_l[hv], in_l[hv]) for hv in range(nv)]
    rhs_l = [jnp.concatenate([betas[hv] * cv[:, hv * DN_DV:(hv + 1) * DN_DV],
                              (betas[hv] * jnp.exp(gc_l[hv])) * kn_l[hv // 2]], axis=1).astype(BF16)
             for hv in range(nv)]
    sol_l = [_dot(tinv_l[hv].astype(BF16), rhs_l[hv]) for hv in range(nv)]
    u0_l = [sol[:, :DN_DV] for sol in sol_l]
    w_l = [sol[:, DN_DV:].astype(BF16) for sol in sol_l]
    qkd_l = [(qk_l[hv // 2] * dec_l[hv]).astype(BF16) for hv in range(nv)]
    qg_l = [(qn_l[hv // 2] * jnp.exp(gc_l[hv])).astype(BF16) for hv in range(nv)]
    kt_l = [(kn_l[hv // 2] * jnp.exp(gl_l[hv] - gc_l[hv])).astype(BF16) for hv in range(nv)]
    o_parts = [[] for _ in range(nv)]
    for ci in range(nch_eff):
        rows = slice(ci * c, (ci + 1) * c)
        s_old_l = [s_scr[hv] for hv in range(nv)]
        sb_l = [s.astype(BF16) for s in s_old_l]
        ub_l = [(u0_l[hv][rows] - _dot(w_l[hv][rows], sb_l[hv])).astype(BF16) for hv in range(nv)]
        for hv in range(nv):
            u_scr[hv, rows, :] = ub_l[hv]
        for hv in range(nv):
            o_parts[hv].append(_dot(qg_l[hv][rows], sb_l[hv]) + _dot(qkd_l[hv][rows], u_scr[hv]))
        for hv in range(nv):
            s_scr[hv] = (jnp.exp(gl_l[hv][ci * c:ci * c + 1, :]) * s_old_l[hv]
                         + _dot_tn(kt_l[hv][rows], ub_l[hv]))
    for hv in range(nv):
        vsl = slice(hv * DN_DV, (hv + 1) * DN_DV)
        o = (jnp.concatenate(o_parts[hv], axis=0) if nch_eff > 1 else o_parts[hv][0])[:rb]
        y = o * lax.rsqrt(jnp.mean(o * o, axis=-1, keepdims=True) + EPS) * ng_ref[...]
        o_ref[0, :, vsl] = (y * _silu(z_ref[0, :, vsl])).astype(o_ref.dtype)

    @pl.when(grp == pl.num_programs(2) - 1)
    def _():
        sfin_ref[0] = s_scr[...]


def delta_branch(p_main, p_ba, conv_past, conv_w, layer, alog128, dtb128, ng, s0, *, nch, out_dtype):
    b, l, _ = p_main.shape
    c = DN_CHUNK
    r = c * nch
    rb = min(r, l)
    assert l % rb == 0
    ngrp = l // rb
    hb = DELTA_HEADS_PER_STEP
    qw, vw = hb * DN_DK, 2 * hb * DN_DV
    tri, lev = _delta_consts(c, nch)
    kq, kv, kz = DN_QW // qw, (2 * DN_QW) // vw, CONV_CH // vw
    in_specs = [
        pl.BlockSpec((1, rb, qw), lambda i, h, g: (i, g, h)),
        pl.BlockSpec((1, rb, qw), lambda i, h, g: (i, g, kq + h)),
        pl.BlockSpec((1, rb, vw), lambda i, h, g: (i, g, kv + h)),
        pl.BlockSpec((1, rb, vw), lambda i, h, g: (i, g, kz + h)),
        pl.BlockSpec((1, rb, LANE), lambda i, h, g: (i, g, 0)),
        pl.BlockSpec((1, DN_CONV - 1, qw), lambda i, h, g: (i, 0, h)),
        pl.BlockSpec((1, DN_CONV - 1, qw), lambda i, h, g: (i, 0, kq + h)),
        pl.BlockSpec((1, DN_CONV - 1, vw), lambda i, h, g: (i, 0, kv + h)),
        pl.BlockSpec((None, DN_CONV, qw), lambda i, h, g: (layer, 0, h)),
        pl.BlockSpec((None, DN_CONV, qw), lambda i, h, g: (layer, 0, kq + h)),
        pl.BlockSpec((None, DN_CONV, vw), lambda i, h, g: (layer, 0, kv + h)),
        pl.BlockSpec((1, LANE), lambda i, h, g: (0, 0)),
        pl.BlockSpec((1, LANE), lambda i, h, g: (0, 0)),
        pl.BlockSpec((1, DN_DV), lambda i, h, g: (0, 0)),
        pl.BlockSpec((1, 2 * hb, DN_DK, DN_DV), lambda i, h, g: (i, h, 0, 0)),
        pl.BlockSpec(tri.shape, lambda i, h, g: (0, 0, 0)),
        pl.BlockSpec(lev.shape, lambda i, h, g: (0, 0, 0)),
    ]
    out_specs = [pl.BlockSpec((1, rb, vw), lambda i, h, g: (i, g, h)),
                 pl.BlockSpec((1, 2 * hb, DN_DK, DN_DV), lambda i, h, g: (i, h, 0, 0))]
    out_shape = [jax.ShapeDtypeStruct((b, l, DN_VW), out_dtype),
                 jax.ShapeDtypeStruct((b, DN_HV, DN_DK, DN_DV), F32)]
    scratch = [pltpu.VMEM((SUBLANE + r, qw), F32), pltpu.VMEM((SUBLANE + r, qw), F32),
               pltpu.VMEM((SUBLANE + r, vw), F32), pltpu.VMEM((r, LANE), F32),
               pltpu.VMEM((2 * hb, DN_DK, DN_DV), F32), pltpu.VMEM((2 * hb, r, DN_DV), BF16)]
    return pl.pallas_call(
        functools.partial(_delta_kernel, l=l, c=c, nch=nch, hb=hb),
        grid=(b, DN_HK // hb, ngrp),
        in_specs=in_specs, out_specs=out_specs, out_shape=out_shape, scratch_shapes=scratch,
        compiler_params=_cparams(("parallel", "parallel", "arbitrary")),
        name="delta_branch",
    )(p_main, p_main, p_main, p_main, p_ba, conv_past, conv_past, conv_past,
      conv_w, conv_w, conv_w, alog128, dtb128, ng, s0, tri, lev)


def _rope_tables(pos):
    half = ROT_DIM // 2
    inv = 1.0 / (ROPE_THETA ** (jnp.arange(0, ROT_DIM, 2, dtype=F32) / ROT_DIM))
    ang = pos[:, None] * inv[None, :]
    cos, sin = jnp.cos(ang), jnp.sin(ang)
    n = pos.shape[0]
    zh = jnp.zeros((n, half), F32)
    rest = SW_HD - ROT_DIM
    ctab = jnp.concatenate([cos, cos, jnp.ones((n, rest), F32)], axis=1)
    s_hi = jnp.concatenate([zh, sin, jnp.zeros((n, rest), F32)], axis=1)
    s_lo = jnp.concatenate([-sin, zh, jnp.zeros((n, rest), F32)], axis=1)
    two = lambda t: jnp.concatenate([t, t], axis=1)
    return two(ctab), two(s_hi), two(s_lo)


def _swa_kernel(*refs, qb, qbp, pos0, multi):
    if multi:
        (q_ref, kc_ref, vc_ref, kp_ref, vp_ref, kpast_ref, vpast_ref, ct_ref, sh_ref, sl_ref,
         ctp_ref, shp_ref, slp_ref, qg_ref, kg_ref, sink_ref, o_ref, knew_ref,
         qbuf, kbuf, vbuf, qr_buf, kr_buf) = refs
    else:
        (q_ref, kc_ref, vc_ref, kpast_ref, vpast_ref, ct_ref, sh_ref, sl_ref,
         qg_ref, kg_ref, sink_ref, o_ref, knew_ref, qbuf, kbuf, vbuf, qr_buf, kr_buf) = refs
    blk = pl.program_id(1)
    g = SW_HQ // SW_HKV
    w = WINDOW
    half = ROT_DIM // 2

    if qbp > qb:
        qbuf[...] = jnp.zeros_like(qbuf)
        kbuf[...] = jnp.zeros_like(kbuf)
        vbuf[...] = jnp.zeros_like(vbuf)
    qbuf[0:qb, :] = q_ref[0]
    kbuf[0:qb, :] = kc_ref[0]
    vbuf[0:qb, :] = vc_ref[0]

    def rope(x, gain, ct, sh, sl):
        xg = x * gain
        return xg * ct + pltpu.roll(xg, half, 1) * sh + pltpu.roll(xg, LANE - half, 1) * sl

    ct, sh, sl = ct_ref[...], sh_ref[...], sl_ref[...]
    for cc in range(SW_QW // LANE):
        csl = slice(cc * LANE, (cc + 1) * LANE)
        qr_buf[:, csl] = rope(qbuf[:, csl], qg_ref[...], ct, sh, sl)
    for cc in range(SW_KW // LANE):
        csl = slice(cc * LANE, (cc + 1) * LANE)
        kr_buf[w:w + qbp, csl] = rope(kbuf[:, csl], kg_ref[...], ct, sh, sl)
        if multi:
            kr_buf[0:w, csl] = rope(kp_ref[0, :, csl], kg_ref[...], ctp_ref[...], shp_ref[...], slp_ref[...])

    def rms(x):
        return lax.rsqrt(jnp.mean(x * x, axis=-1, keepdims=True) + EPS)

    row = lax.broadcasted_iota(jnp.int32, (g * qbp, w + qbp), 0) & (qbp - 1)
    col = lax.broadcasted_iota(jnp.int32, (g * qbp, w + qbp), 1)
    kpos_ok = (pos0 + blk * qb - w + col) >= 0
    allowed = (col <= row + w) & (col > row) & kpos_ok

    heads = range(SW_HKV)
    kslab_l, vslab_l, qs_l, sink_l = [], [], [], []
    for kh in heads:
        ksl = slice(kh * SW_HD, (kh + 1) * SW_HD)
        k_cur = kr_buf[w:w + qbp, ksl] * rms(kbuf[:, ksl])
        knew_ref[0, :, ksl] = k_cur[:qb]
        v_cur = vbuf[:, ksl]
        if multi:
            k_prev_blk = kr_buf[0:w, ksl] * rms(kp_ref[0, :, ksl])
            first = blk == 0
            k_prev = jnp.where(first, kpast_ref[0, :, ksl], k_prev_blk)
            v_prev = jnp.where(first, vpast_ref[0, :, ksl], vp_ref[0, :, ksl])
        else:
            k_prev = kpast_ref[0, :, ksl]
            v_prev = vpast_ref[0, :, ksl]
        kslab_l.append(jnp.concatenate([k_prev, k_cur], axis=0).astype(BF16))
        vslab_l.append(jnp.concatenate([v_prev, v_cur], axis=0).astype(BF16))
        qs, sinks = [], []
        for gi in range(g):
            hq = kh * g + gi
            qsl = slice(hq * SW_HD, (hq + 1) * SW_HD)
            qs.append(qr_buf[:, qsl] * (rms(qbuf[:, qsl]) * (SW_HD ** -0.5)))
            sinks.append(jnp.broadcast_to(sink_ref[0:1, hq:hq + 1], (qbp, 1)))
        qs_l.append(jnp.concatenate(qs, axis=0).astype(BF16))
        sink_l.append(jnp.concatenate(sinks, axis=0))
    s_l = [jnp.where(allowed, _dot_nt(qs_l[kh], kslab_l[kh]), NEG) for kh in heads]
    m_l = [jnp.maximum(jnp.max(s_l[kh], axis=-1, keepdims=True), sink_l[kh]) for kh in heads]
    p_l = [jnp.exp(s_l[kh] - m_l[kh]) for kh in heads]
    den_l = [jnp.sum(p_l[kh], axis=-1, keepdims=True) + jnp.exp(sink_l[kh] - m_l[kh]) for kh in heads]
    o_l = [_dot(p_l[kh].astype(BF16), vslab_l[kh]) / den_l[kh] for kh in heads]
    for kh in heads:
        for gi in range(g):
            hq = kh * g + gi
            o_ref[0, :, hq * SW_HD:(hq + 1) * SW_HD] = o_l[kh][gi * qbp:gi * qbp + qb].astype(o_ref.dtype)


def swa_branch(p_tail, k_past, v_past, qn_g, kn_g, sinks, pos0, *, out_dtype):
    b, l, _ = p_tail.shape
    qb = min(WINDOW, l)
    qbp = -(-qb // SUBLANE) * SUBLANE
    nb = l // qb
    multi = nb > 1
    assert qbp & (qbp - 1) == 0 and (multi is False or qbp == qb)
    tabs = _rope_tables(pos0 + jnp.arange(l, dtype=F32))
    if qbp > qb:
        tabs = tuple(jnp.pad(t, ((0, qbp - qb), (0, 0))) for t in tabs)
    kcb, vcb = SW_QW // SW_KW, SW_QW // SW_KW + 1
    cur = lambda i, j: (i, j, 0)
    tab_cur = pl.BlockSpec((qbp, LANE), lambda i, j: (j, 0))
    tab_prev = pl.BlockSpec((qbp, LANE), lambda i, j: (jnp.maximum(j - 1, 0), 0))
    in_specs = [pl.BlockSpec((1, qb, SW_QW), cur),
                pl.BlockSpec((1, qb, SW_KW), lambda i, j: (i, j, kcb)),
                pl.BlockSpec((1, qb, SW_KW), lambda i, j: (i, j, vcb))]
    args = [p_tail, p_tail, p_tail]
    if multi:
        in_specs += [pl.BlockSpec((1, qb, SW_KW), lambda i, j: (i, jnp.maximum(j - 1, 0), kcb)),
                     pl.BlockSpec((1, qb, SW_KW), lambda i, j: (i, jnp.maximum(j - 1, 0), vcb))]
        args += [p_tail, p_tail]
    in_specs += [pl.BlockSpec((1, WINDOW, SW_KW), lambda i, j: (i, 0, 0)),
                 pl.BlockSpec((1, WINDOW, SW_KW), lambda i, j: (i, 0, 0)),
                 tab_cur, tab_cur, tab_cur]
    args += [k_past, v_past, *tabs]
    if multi:
        in_specs += [tab_prev, tab_prev, tab_prev]
        args += list(tabs)
    in_specs += [pl.BlockSpec((1, LANE), lambda i, j: (0, 0)),
                 pl.BlockSpec((1, LANE), lambda i, j: (0, 0)),
                 pl.BlockSpec((1, SW_HQ), lambda i, j: (0, 0))]
    two = lambda gvec: jnp.concatenate([gvec, gvec]).reshape(1, LANE)
    args += [two(qn_g), two(kn_g), sinks.reshape(1, SW_HQ)]
    return pl.pallas_call(
        functools.partial(_swa_kernel, qb=qb, qbp=qbp, pos0=pos0, multi=multi),
        grid=(b, nb),
        in_specs=in_specs,
        out_specs=[pl.BlockSpec((1, qb, SW_QW), cur), pl.BlockSpec((1, qb, SW_KW), cur)],
        out_shape=[jax.ShapeDtypeStruct((b, l, SW_QW), out_dtype), jax.ShapeDtypeStruct((b, l, SW_KW), F32)],
        scratch_shapes=[pltpu.VMEM((qbp, SW_QW), F32), pltpu.VMEM((qbp, SW_KW), F32), pltpu.VMEM((qbp, SW_KW), F32),
                        pltpu.VMEM((qbp, SW_QW), F32), pltpu.VMEM((WINDOW + qbp, SW_KW), F32)],
        compiler_params=_cparams(("parallel", "arbitrary")),
        name="swa_branch",
    )(*args)


_PEER_CAND = [(a, b) for a in range(PEER_TOPK) for b in range(PEER_TOPK) if (a + 1) * (b + 1) <= PEER_TOPK]
_PEER_CAND_ROWS = -(-len(_PEER_CAND) // SUBLANE) * SUBLANE


def _extract_top(work, nrow, k):
    rowid = lax.broadcasted_iota(jnp.int32, work.shape, 0)
    rank = jnp.full(work.shape, nrow, jnp.int32)
    vals = []
    for r in range(k):
        m = jnp.max(work, axis=0, keepdims=True)
        vals.append(m)
        first = jnp.min(jnp.where(work == m, rowid, nrow), axis=0, keepdims=True)
        hit = rowid == first
        rank = jnp.where(hit, r, rank)
        work = jnp.where(hit, -jnp.inf, work)
    return vals, rank


def _peer_topk_kernel(q_ref, sk_ref, amap_ref, bmap_ref, r1_ref, e1_ref, n0_ref, w0_ref):
    kk = PEER_TOPK
    amap = amap_ref[...]
    bmap = bmap_ref[...]
    for h in range(PEER_HEADS):
        sc, tops, ranks = [], [], []
        for s in range(2):
            col = (h * 2 + s) * PEER_NKEYS
            sc_t = _dot_nt(sk_ref[h, s], q_ref[:, col:col + PEER_NKEYS], HIGHEST)
            vals, rank = _extract_top(sc_t, PEER_NKEYS, kk)
            sc.append(sc_t)
            tops.append(vals)
            ranks.append(rank)
        a0, a1 = tops
        ca = jnp.full(amap.shape, -jnp.inf, F32)
        cb = jnp.zeros(amap.shape, F32)
        for r in range(kk):
            ca = jnp.where(amap == r, a0[r], ca)
            cb = jnp.where(bmap == r, a1[r], cb)
        best, _ = _extract_top(ca + cb, _PEER_CAND_ROWS, kk)
        mx = best[0]
        den = jnp.zeros_like(mx)
        for r in range(kk):
            den = den + jnp.exp(best[r] - mx)
        thr = best[kk - 1]
        cnt = jnp.zeros(sc[0].shape, F32)
        for r in range(kk):
            cnt = cnt + jnp.where(sc[0] + a1[r] >= thr, 1.0, 0.0)
        r1_ref[h] = ranks[1].astype(F32).astype(r1_ref.dtype)
        e1_ref[h] = jnp.exp(sc[1] - a1[0]).astype(e1_ref.dtype)
        n0_ref[h] = jnp.where(ranks[0] < kk, cnt, 0.0)
        w0_ref[h] = jnp.exp(sc[0] - a0[0]) / den


def peer_topk(q, subkeys, layer, tb):
    t = q.shape[0]
    amap = np.full((_PEER_CAND_ROWS, 1), -1, np.int32)
    bmap = np.full((_PEER_CAND_ROWS, 1), -1, np.int32)
    amap[:len(_PEER_CAND), 0] = [a for a, _ in _PEER_CAND]
    bmap[:len(_PEER_CAND), 0] = [b for _, b in _PEER_CAND]
    amap = jnp.asarray(np.broadcast_to(amap, (_PEER_CAND_ROWS, tb)))
    bmap = jnp.asarray(np.broadcast_to(bmap, (_PEER_CAND_ROWS, tb)))
    shape = (PEER_HEADS, PEER_NKEYS, t)
    bspec = pl.BlockSpec((PEER_HEADS, PEER_NKEYS, tb), lambda i: (0, 0, i))
    cspec = pl.BlockSpec((_PEER_CAND_ROWS, tb), lambda i: (0, 0))
    return pl.pallas_call(
        _peer_topk_kernel,
        grid=(t // tb,),
        in_specs=[pl.BlockSpec((tb, q.shape[1]), lambda i: (i, 0)),
                  pl.BlockSpec((None,) + subkeys.shape[1:], lambda i: (layer, 0, 0, 0, 0)),
                  cspec, cspec],
        out_specs=[bspec, bspec, bspec, bspec],
        out_shape=[jax.ShapeDtypeStruct(shape, BF16), jax.ShapeDtypeStruct(shape, BF16),
                   jax.ShapeDtypeStruct(shape, F32), jax.ShapeDtypeStruct(shape, F32)],
        compiler_params=_cparams(("parallel",)),
        name="peer_topk",
    )(q, subkeys, amap, bmap)


def _gelu_tanh(x):
    return 0.5 * x * (1.0 + jnp.tanh(math.sqrt(2.0 / math.pi) * (x + 0.044715 * (x * x * x))))


def _peer_mix_kernel(h_ref, u_ref, v_ref, r1_ref, e1_ref, n0_ref, w0_ref, x_ref, g_ref,
                     o_ref, acc_ref, p_scr, *, et):
    e = pl.program_id(1)
    nt = pl.num_programs(1) - 1

    @pl.when(e == 0)
    def _():
        acc_ref[...] = jnp.zeros_like(acc_ref)
        p_scr[1] = jnp.zeros(p_scr.shape[1:], p_scr.dtype)

    tile = jnp.minimum(e, nt - 1)
    d = acc_ref.shape[1]
    npiece = PEER_MIX_PIECES
    rows_per = et // npiece
    cols_per = d // npiece
    prev = (e + 1) & 1
    cur = e & 1
    for pc in range(npiece):
        csl = slice(pc * cols_per, (pc + 1) * cols_per)
        acc_ref[:, csl] += _dot_tn(p_scr[prev], v_ref[:, csl])
        rsl = slice(pc * rows_per, (pc + 1) * rows_per)
        act = _gelu_tanh(_dot_nt(u_ref[rsl, :], h_ref[...])).astype(BF16)
        for ii in range(rows_per // PEER_NKEYS):
            i = tile * (et // PEER_NKEYS) + pc * (rows_per // PEER_NKEYS) + ii
            gate = None
            for h in range(PEER_HEADS):
                n0 = n0_ref[h, pl.ds(i, 1), :].astype(BF16)
                w0 = w0_ref[h, pl.ds(i, 1), :].astype(BF16)
                term = jnp.where(r1_ref[h] < n0, e1_ref[h] * w0, jnp.zeros((), BF16))
                gate = term if gate is None else gate + term
            lo = pc * rows_per + ii * PEER_NKEYS
            p_scr[cur, lo:lo + PEER_NKEYS, :] = gate * act[ii * PEER_NKEYS:(ii + 1) * PEER_NKEYS]

    @pl.when(e == nt)
    def _():
        o_ref[...] = x_ref[...] + g_ref[0] * acc_ref[...]


def peer_mix(h2, peer_u, peer_v, layer, r1t, e1t, n0t, w0t, xres, gate, rows_per_seq, *, tb, et):
    t, d = h2.shape
    ne = peer_u.shape[1]
    nt = ne // et
    once = pl.Buffered(1)
    if gate.shape[1] == 1:
        assert rows_per_seq % tb == 0
        per = rows_per_seq // tb
        gspec = pl.BlockSpec((1, 1, d), lambda i, e: (i // per, 0, 0))
    else:
        gspec = pl.BlockSpec((1, tb, d), lambda i, e: (0, i, 0))
    bspec = pl.BlockSpec((PEER_HEADS, PEER_NKEYS, tb), lambda i, e: (0, 0, i), pipeline_mode=once)
    return pl.pallas_call(
        functools.partial(_peer_mix_kernel, et=et),
        grid=(t // tb, nt + 1),
        in_specs=[pl.BlockSpec((tb, d), lambda i, e: (i, 0)),
                  pl.BlockSpec((None, et, d), lambda i, e: (layer, jnp.minimum(e, nt - 1), 0)),
                  pl.BlockSpec((None, et, d), lambda i, e: (layer, jnp.maximum(e - 1, 0), 0)),
                  bspec, bspec, bspec, bspec,
                  pl.BlockSpec((tb, d), lambda i, e: (i, 0), pipeline_mode=once),
                  gspec],
        out_specs=pl.BlockSpec((tb, d), lambda i, e: (i, 0)),
        out_shape=jax.ShapeDtypeStruct((t, d), F32),
        scratch_shapes=[pltpu.VMEM((tb, d), F32), pltpu.VMEM((2, et, tb), BF16)],
        compiler_params=_cparams(("parallel", "arbitrary")),
        name="peer_mix",
    )(h2, peer_u, peer_v, r1t, e1t, n0t, w0t, xres, gate)


def _expand_rows(m, reps):
    return jnp.repeat(m, reps, axis=0)[None]


def trunk_layer(x, mods, p, layer, conv_past, s0, k_past, v_past, pos0, *, per_row):
    b, l, d = x.shape
    t = b * l
    sh1, sc1, g1, sh2, sc2, g2 = mods
    if per_row:
        expand = lambda m: _expand_rows(m, l)
        xn, mod_l, tl = x.reshape(1, t, d), t, t
    else:
        expand = lambda m: m[:, None, :]
        xn, mod_l, tl = x, l, min(512, l)
    sh1, sc1, g1, sh2, sc2, g2 = [expand(m) for m in (sh1, sc1, g1, sh2, sc2, g2)]
    seq_dtype = F32 if per_row else BF16

    h = norm_modulate(xn, p['norm1_g'][layer], sc1, sh1, tl).reshape(t, d)
    p_main = matmul(h, p['w_in'], layer, 0, MAIN_COLS)
    p_ba = matmul(h, p['w_ba'], layer, 0, LANE, tn=LANE, precision=HIGHEST)
    p_tail = matmul(h, p['w_tail'], layer, 0, TAIL_COLS)

    o_dn, s_fin = delta_branch(p_main.reshape(b, l, MAIN_COLS), p_ba.reshape(b, l, LANE), conv_past,
                               p['conv_w'], layer, p['alog128'][layer], p['dtb128'][layer],
                               p['dn_norm_g'][layer].reshape(1, DN_DV), s0,
                               nch=2 if per_row else 4, out_dtype=seq_dtype)
    y_a = matmul(o_dn.reshape(t, DN_VW), p['w_branch_a'], layer, 0, d)

    attn, k_new = swa_branch(p_tail.reshape(b, l, TAIL_COLS), k_past, v_past,
                             p['qn_g'][layer], p['kn_g'][layer], p['sinks'][layer], pos0, out_dtype=seq_dtype)
    ga0 = SW_QW + 2 * SW_KW
    merged = matmul_merge(attn.reshape(t, SW_QW), p['w_branch_b'], layer, p_tail, ga0, ga0 + d, y_a)
    x1 = matmul_resid(merged, p['w_out'], layer, x.reshape(t, d), g1, mod_l)

    h2 = norm_modulate(x1.reshape(xn.shape), p['norm2_g'][layer], sc2, sh2, tl).reshape(t, d)
    q = matmul(h2, p['peer_wq'], layer, 0, d)
    tbp = 128 if t <= 128 else 256
    r1t, e1t, n0t, w0t = peer_topk(q, p['peer_subkeys'], layer, tbp)
    tb = min(512, t)
    x2 = peer_mix(h2, p['peer_u'], p['peer_v'], layer, r1t, e1t, n0t, w0t, x1, g2, mod_l, tb=tb, et=1024)

    new_conv = p_main.reshape(b, l, MAIN_COLS)[:, :, :CONV_CH]
    sv = p_tail.reshape(b, l, TAIL_COLS)[:, :, SW_QW + SW_KW:SW_QW + 2 * SW_KW]
    return x2.reshape(b, l, d), new_conv, s_fin, k_new, sv


def kernel(x_prompt, x_sample, state_conv, state_delta, cache_swa_k, cache_swa_v, c_prompt, c_sample,
           norm1_g, norm2_g, w_mod, b_mod, w_in, conv_w, a_log, dt_bias, dn_norm_g, qn_g, kn_g, sinks,
           w_branch_a, w_branch_b, w_out, peer_wq, peer_subkeys, peer_u, peer_v):
    depth = w_in.shape[0]
    bp, lp, d = x_prompt.shape
    bs, ls, _ = x_sample.shape
    wb = cache_swa_k.shape[2]
    assert wb == WINDOW

    c_all = jnp.concatenate([c_prompt, c_sample], axis=0)
    mpad = -(-c_all.shape[0] // SUBLANE) * SUBLANE
    c_all = jnp.pad(c_all, ((0, mpad - c_all.shape[0]), (0, 0)))
    zero_conv = jnp.zeros((bp, DN_CONV - 1, CONV_CH), F32)
    zero_delta = jnp.zeros((bp, DN_HV, DN_DK, DN_DV), F32)
    zero_kv = jnp.zeros((bp, wb, SW_KW), F32)

    lane_pad = lambda v: jnp.pad(v, ((0, 0), (DN_HV, LANE - 2 * DN_HV)))[:, None, :]
    p = dict(norm1_g=norm1_g, norm2_g=norm2_g, w_in=w_in,
             w_ba=jnp.pad(w_in[:, :, MAIN_COLS:MAIN_COLS + BA_COLS], ((0, 0), (0, 0), (0, LANE - BA_COLS))),
             w_tail=w_in[:, :, MAIN_COLS + BA_COLS:],
             conv_w=conv_w, alog128=lane_pad(a_log), dtb128=lane_pad(dt_bias),
             dn_norm_g=dn_norm_g, qn_g=qn_g, kn_g=kn_g, sinks=sinks,
             w_branch_a=w_branch_a, w_branch_b=w_branch_b, w_out=w_out, peer_wq=peer_wq,
             peer_subkeys=peer_subkeys, peer_u=peer_u.astype(BF16), peer_v=peer_v.astype(BF16))

    yp, ys = x_prompt, x_sample
    outs = [[] for _ in range(8)]
    for l in range(depth):
        mod = modulation(c_all, w_mod, b_mod, l)
        mods_p = jnp.split(mod[:bp], 6, axis=-1)
        mods_s = jnp.split(mod[bp:bp + bs], 6, axis=-1)
        yp, pre_p, pd, pk, pv = trunk_layer(yp, mods_p, p, l, zero_conv, zero_delta, zero_kv, zero_kv, 0,
                                            per_row=False)
        ys, pre_s, sd, sk, sv = trunk_layer(ys, mods_s, p, l, state_conv[l], state_delta[l],
                                            cache_swa_k[l].reshape(bs, wb, SW_KW),
                                            cache_swa_v[l].reshape(bs, wb, SW_KW), PAST_LEN, per_row=True)
        keep = DN_CONV - 1
        outs[0].append(pre_p[:, lp - keep:])
        outs[1].append(pd)
        outs[2].append(pk[:, lp - wb:].reshape(bp, wb, SW_HKV, SW_HD))
        outs[3].append(pv[:, lp - wb:].reshape(bp, wb, SW_HKV, SW_HD))
        outs[4].append(jnp.concatenate([state_conv[l], pre_s], axis=1)[:, -keep:])
        outs[5].append(sd)
        outs[6].append(jnp.concatenate([cache_swa_k[l], sk.reshape(bs, ls, SW_HKV, SW_HD)], axis=1)[:, -wb:])
        outs[7].append(jnp.concatenate([cache_swa_v[l], sv.reshape(bs, ls, SW_HKV, SW_HD)], axis=1)[:, -wb:])
    return (yp, ys) + tuple(jnp.stack(o) for o in outs)
```
